```python
import math
import jax, jax.numpy as jnp
from jax import lax
import numpy as np

D_MODEL = 1024
BATCH = 2
SEQ = 8192
DEPTH = 4
DEC_BATCH = 128
DEC_SEQ = 4
PAST_LEN = 8192
PAGE_SIZE = 128

H_A = 4
DK_A = 32
DV_A = 64
RET_CHUNK = 128
RET_THETA = 10000.0
H_B = 8
KV_B = 2
G_B = H_B // KV_B
HD_B = 64
WINDOW = 128
ROPE_THETA = 500000.0
ROT_B = HD_B // 4
W_C = 256
POOL_WINDOWS = (2, 4, 8, 16)
N_POOL = 4
GC = W_C // N_POOL
POOL_HIST = 15
H_D = 4
DK_D = 64
DV_D = 64
HGRN_CHUNK = 16

W_A = H_A * DV_A
W_B = H_B * HD_B
W_D = H_D * DV_D
N_BRANCH = 4
BRANCH_WIDTHS = (W_A, W_B, W_C, W_D)
W_MIX = W_A + W_B + W_C + W_D
IN_WIDTHS = (
    H_A * DK_A, H_A * DK_A, W_A, W_A,
    H_B * HD_B, KV_B * HD_B, KV_B * HD_B, W_B,
    W_C, W_C,
    H_D * DK_D, H_D * DK_D, W_D, W_D,
    N_BRANCH * D_MODEL,
)
D_IN = sum(IN_WIDTHS)
EPS = 1e-6

kernel_name = "hybrid_gated_parallel_decoder_step"


def rms_norm(x, g=None):
    xf = x.astype(jnp.float32)
    y = xf * lax.rsqrt(jnp.mean(xf * xf, axis=-1, keepdims=True) + EPS)
    if g is not None:
        y = y * g.astype(jnp.float32)
    return y.astype(x.dtype)


def rope(x, pos, rot_dim, theta):
    half = rot_dim // 2
    inv = jnp.power(theta, -jnp.arange(half, dtype=jnp.float32) / half)
    ang = pos.astype(jnp.float32)[:, None] * inv[None, :]
    cos = jnp.cos(ang)[None, :, None, :]
    sin = jnp.sin(ang)[None, :, None, :]
    xf = x.astype(jnp.float32)
    x1 = xf[..., :half]
    x2 = xf[..., half:rot_dim]
    out = jnp.concatenate([x1 * cos - x2 * sin, x2 * cos + x1 * sin, xf[..., rot_dim:]], axis=-1)
    return out.astype(x.dtype)


def retention(q, k, v, s0):
    B, T, H, DK = q.shape
    DV = v.shape[-1]
    C = math.gcd(T, RET_CHUNK)
    n = T // C
    f32 = jnp.float32
    q = q.astype(f32).reshape(B, n, C, H, DK)
    k = k.astype(f32).reshape(B, n, C, H, DK)
    v = v.astype(f32).reshape(B, n, C, H, DV)
    lg = jnp.log1p(-jnp.power(2.0, -5.0 - jnp.arange(H, dtype=f32)))
    idx = jnp.arange(C, dtype=f32)
    rel = idx[:, None] - idx[None, :]
    dmask = jnp.where(rel[None] >= 0, jnp.exp(jnp.maximum(rel, 0.0)[None] * lg[:, None, None]), 0.0)
    scores = jnp.einsum('bnthd,bnshd->bnhts', q, k) * dmask[None, None]
    o = jnp.einsum('bnhts,bnshe->bnthe', scores, v)
    q_dec = q * jnp.exp((idx[:, None] + 1.0) * lg[None, :])[:, :, None]
    k_dec = k * jnp.exp((C - 1.0 - idx)[:, None] * lg[None, :])[:, :, None]
    kv = jnp.einsum('bnshd,bnshe->nbhde', k_dec, v)
    cdec = jnp.exp(C * lg)[None, :, None, None]

    def step(s, kv_n):
        return cdec * s + kv_n, s

    s_fin, s_starts = lax.scan(step, s0.astype(f32), kv)
    o = o + jnp.einsum('bnthd,nbhde->bnthe', q_dec, s_starts)
    return o.reshape(B, T, H, DV), s_fin


def hgrn2_scan(q, k, v, log_f, s0):
    B, T, H, DK = q.shape
    DV = v.shape[-1]
    C = math.gcd(T, HGRN_CHUNK)
    n = T // C
    f32 = jnp.float32
    q = q.astype(f32).reshape(B, n, C, H, DK)
    k = k.astype(f32).reshape(B, n, C, H, DK)
    v = v.astype(f32).reshape(B, n, C, H, DV)
    b = jnp.cumsum(log_f.astype(f32).reshape(B, n, C, H, DK), axis=2)
    causal = jnp.tril(jnp.ones((C, C), dtype=bool))
    diff = b[:, :, :, None] - b[:, :, None, :]
    w = jnp.exp(jnp.where(causal[None, None, :, :, None, None], diff, -jnp.inf))
    scores = jnp.einsum('bnthd,bntshd,bnshd->bnhts', q, w, k)
    o = jnp.einsum('bnhts,bnshe->bnthe', scores, v)
    b_last = b[:, :, -1]
    q_dec = q * jnp.exp(b)
    k_dec = k * jnp.exp(b_last[:, :, None] - b)
    kv = jnp.einsum('bnshd,bnshe->nbhde', k_dec, v)
    cdec = jnp.moveaxis(jnp.exp(b_last), 1, 0)[..., None]

    def step(s, xs):
        dec, kv_n = xs
        return dec * s + kv_n, s

    s_fin, s_starts = lax.scan(step, s0.astype(f32), (cdec, kv))
    o = o + jnp.einsum('bnthd,nbhde->bnthe', q_dec, s_starts)
    return o.reshape(B, T, H, DV), s_fin


def swa_sink_attention(q, k, v, k_buf, v_buf, sink, start):
    B, T = q.shape[:2]
    Bq = WINDOW if T % WINDOW == 0 else T
    nb = T // Bq
    L = WINDOW + Bq
    k_ext = jnp.concatenate([k_buf.astype(k.dtype), k], axis=1)
    v_ext = jnp.concatenate([v_buf.astype(v.dtype), v], axis=1)
    ctx = jnp.arange(nb)[:, None] * Bq + jnp.arange(L)[None, :]
    kb = k_ext[:, ctx].astype(jnp.float32)
    vb = v_ext[:, ctx].astype(jnp.float32)
    qb = q.astype(jnp.float32).reshape(B, nb, Bq, KV_B, G_B, HD_B)
    q_pos = start + jnp.arange(T).reshape(nb, Bq)
    k_pos = start - WINDOW + ctx
    dist = q_pos[:, :, None] - k_pos[:, None, :]
    mask = (dist >= 0) & (dist <= WINDOW) & (k_pos[:, None, :] >= 0)
    s = jnp.einsum('bnqkgd,bnlkd->bnkgql', qb, kb) * (HD_B ** -0.5)
    s = jnp.where(mask[None, :, None, None], s, -jnp.inf)
    sink_col = jnp.broadcast_to(sink.astype(jnp.float32).reshape(KV_B, G_B)[None, None, :, :, None, None],
                                s.shape[:-1] + (1,))
    p = jax.nn.softmax(jnp.concatenate([s, sink_col], axis=-1), axis=-1)[..., :-1]
    o = jnp.einsum('bnkgql,bnlkd->bnqkgd', p, vb)
    return o.reshape(B, T, H_B * HD_B), k_ext[:, -WINDOW:], v_ext[:, -WINDOW:]


def pool_mix(u, hist, w_pool, pool_scale, start):
    B, T, _ = u.shape
    ext_raw = jnp.concatenate([hist.astype(u.dtype), u], axis=1)
    ext = ext_raw.astype(jnp.float32)
    cs = jnp.concatenate([jnp.zeros((B, 1, W_C), jnp.float32), jnp.cumsum(ext, axis=1)], axis=1)
    end = cs[:, POOL_HIST + 1:]
    pos = start + jnp.arange(T)
    outs = []
    for g, w in enumerate(POOL_WINDOWS):
        sl = slice(g * GC, (g + 1) * GC)
        win = end[..., sl] - cs[:, POOL_HIST + 1 - w: POOL_HIST + 1 - w + T, sl]
        cnt = jnp.minimum(pos + 1, w).astype(jnp.float32)[None, :, None]
        outs.append(win / cnt)
    pooled = jnp.concatenate(outs, axis=-1) - u.astype(jnp.float32)
    mixed = jnp.einsum('btgc,gcd->btgd', pooled.reshape(B, T, N_POOL, GC),
                       w_pool.astype(jnp.float32)).reshape(B, T, W_C)
    return mixed * pool_scale.astype(jnp.float32), ext_raw[:, -POOL_HIST:]


def mixer_layer(x, start, s_ret, k_buf, v_buf, p_hist, s_hgrn, lb,
                w_in, w_branch, w_out, g_pre, g_post, sink, w_pool, pool_scale, g_hgrn):
    B, T, _ = x.shape
    dt = x.dtype
    h = rms_norm(x, g_pre)
    proj = h @ w_in
    split_idx = [int(i) for i in np.cumsum(IN_WIDTHS)[:-1]]
    (q_a, k_a, v_a, z_a, q_b, k_b, v_b, z_b, u_c, z_c,
     q_d, f_d, i_d, z_d, gl) = jnp.split(proj, split_idx, axis=-1)
    pos = start + jnp.arange(T)

    qa = rope(q_a.reshape(B, T, H_A, DK_A), pos, DK_A, RET_THETA)
    ka = rope(k_a.reshape(B, T, H_A, DK_A), pos, DK_A, RET_THETA) * (DK_A ** -0.5)
    o_a, s_ret_new = retention(qa, ka, v_a.reshape(B, T, H_A, DV_A), s_ret)
    y_a = (rms_norm(o_a).reshape(B, T, W_A) * jax.nn.silu(z_a.astype(jnp.float32))).astype(dt)

    qb = rope(q_b.reshape(B, T, H_B, HD_B), pos, ROT_B, ROPE_THETA)
    kb = rope(k_b.reshape(B, T, KV_B, HD_B), pos, ROT_B, ROPE_THETA)
    o_b, k_new, v_new = swa_sink_attention(qb, kb, v_b.reshape(B, T, KV_B, HD_B), k_buf, v_buf, sink, start)
    y_b = (o_b * jax.nn.silu(z_b.astype(jnp.float32))).astype(dt)

    o_c, p_new = pool_mix(u_c, p_hist, w_pool, pool_scale, start)
    y_c = (o_c * jax.nn.silu(z_c.astype(jnp.float32))).astype(dt)

    f = lb + (1.0 - lb) * jax.nn.sigmoid(f_d.astype(jnp.float32))
    o_d, s_hgrn_new = hgrn2_scan(jax.nn.silu(q_d.astype(jnp.float32)).reshape(B, T, H_D, DK_D),
                                 (1.0 - f).reshape(B, T, H_D, DK_D),
                                 i_d.reshape(B, T, H_D, DV_D),
                                 jnp.log(f).reshape(B, T, H_D, DK_D), s_hgrn)
    y_d = (rms_norm(o_d, g_hgrn.reshape(H_D, DV_D)).reshape(B, T, W_D)
           * jax.nn.silu(z_d.astype(jnp.float32))).astype(dt)

    gates = jax.nn.sigmoid(gl.reshape(B, T, N_BRANCH, D_MODEL))
    ys = (y_a, y_b, y_c, y_d)
    off = 0
    merged = None
    for i in range(N_BRANCH):
        wd = BRANCH_WIDTHS[i]
        term = gates[:, :, i] * (ys[i] @ w_branch[off:off + wd])
        merged = term if merged is None else merged + term
        off += wd
    out = merged @ w_out
    x_new = x + rms_norm(out, g_post)
    return (x_new, s_ret_new.astype(dt), k_new.astype(dt), v_new.astype(dt),
            p_new.astype(dt), s_hgrn_new.astype(dt))


def trunk(x, start, s_ret, k_buf, v_buf, p_hist, s_hgrn, lb,
          w_in, w_branch, w_out, g_pre, g_post, attn_sink, w_pool, pool_scale, g_hgrn):
    acc = ([], [], [], [], [])
    for l in range(DEPTH):
        x, *st = mixer_layer(x, start, s_ret[l], k_buf[l], v_buf[l], p_hist[l], s_hgrn[l], lb[l],
                             w_in[l], w_branch[l], w_out[l], g_pre[l], g_post[l], attn_sink[l],
                             w_pool[l], pool_scale[l], g_hgrn[l])
        for a, s in zip(acc, st):
            a.append(s)
    return (x, jnp.stack(acc[0]), jnp.stack(acc[1]), jnp.stack(acc[2]),
            jnp.stack(acc[3]), jnp.stack(acc[4]))


def setup_inputs(seed: int = 0) -> dict:
    key = jax.random.key(seed)
    ks = jax.random.split(key, 17)

    def nrm(k, shape, s):
        return jax.random.normal(k, shape, jnp.float32) * s

    return {
        "x_prompt": nrm(ks[0], (BATCH, SEQ, D_MODEL), 1.0),
        "x_sample": nrm(ks[1], (DEC_BATCH, DEC_SEQ, D_MODEL), 1.0),
        "state_ret": nrm(ks[2], (DEPTH, DEC_BATCH, H_A, DK_A, DV_A), 0.5),
        "cache_swa_k": nrm(ks[3], (DEPTH, DEC_BATCH, WINDOW, KV_B, HD_B), 1.0),
        "cache_swa_v": nrm(ks[4], (DEPTH, DEC_BATCH, WINDOW, KV_B, HD_B), 1.0),
        "state_pool": nrm(ks[5], (DEPTH, DEC_BATCH, POOL_HIST, W_C), 1.0),
        "state_hgrn": nrm(ks[6], (DEPTH, DEC_BATCH, H_D, DK_D, DV_D), 0.5),
        "w_in": nrm(ks[7], (DEPTH, D_MODEL, D_IN), D_MODEL ** -0.5),
        "w_branch": nrm(ks[8], (DEPTH, W_MIX, D_MODEL), (W_MIX // N_BRANCH) ** -0.5),
        "w_out": nrm(ks[9], (DEPTH, D_MODEL, D_MODEL), D_MODEL ** -0.5),
        "g_pre": 1.0 + nrm(ks[10], (DEPTH, D_MODEL), 0.1),
        "g_post": 1.0 + nrm(ks[11], (DEPTH, D_MODEL), 0.1),
        "attn_sink": nrm(ks[12], (DEPTH, H_B), 0.5),
        "w_pool": nrm(ks[13], (DEPTH, N_POOL, GC, GC), GC ** -0.5),
        "pool_scale": 1.0 + nrm(ks[14], (DEPTH, W_C), 0.1),
        "g_hgrn": 1.0 + nrm(ks[15], (DEPTH, W_D), 0.1),
        "lower_bounds": nrm(ks[16], (DEPTH, H_D * DK_D), 0.1),
    }


def reference(x_prompt, x_sample, state_ret, cache_swa_k, cache_swa_v, state_pool, state_hgrn,
              w_in, w_branch, w_out, g_pre, g_post, attn_sink, w_pool, pool_scale, g_hgrn,
              lower_bounds):
    lb = jnp.cumsum(jax.nn.softmax(lower_bounds.astype(jnp.float32), axis=0), axis=0)
    lb = lb - lb[0]
    dt = x_prompt.dtype
    y_p, r_p, k_p, v_p, pool_p, h_p = trunk(
        x_prompt, 0,
        jnp.zeros((DEPTH, BATCH, H_A, DK_A, DV_A), dt),
        jnp.zeros((DEPTH, BATCH, WINDOW, KV_B, HD_B), dt),
        jnp.zeros((DEPTH, BATCH, WINDOW, KV_B, HD_B), dt),
        jnp.zeros((DEPTH, BATCH, POOL_HIST, W_C), dt),
        jnp.zeros((DEPTH, BATCH, H_D, DK_D, DV_D), dt),
        lb, w_in, w_branch, w_out, g_pre, g_post, attn_sink, w_pool, pool_scale, g_hgrn)
    y_s, r_s, k_s, v_s, pool_s, h_s = trunk(
        x_sample, PAST_LEN, state_ret, cache_swa_k, cache_swa_v, state_pool, state_hgrn,
        lb, w_in, w_branch, w_out, g_pre, g_post, attn_sink, w_pool, pool_scale, g_hgrn)
    return (y_p, y_s, r_p, k_p, v_p, pool_p, h_p, r_s, k_s, v_s, pool_s, h_s)
```

```python
import functools
import math

import numpy as np
import jax
import jax.numpy as jnp
from jax import lax
from jax.experimental import pallas as pl
from jax.experimental.pallas import tpu as pltpu

F32 = jnp.float32
BF16 = jnp.bfloat16

D_MODEL = 1024
DEPTH = 4
PAST_LEN = 8192
H_A, DK_A, DV_A = 4, 32, 64
RET_CHUNK = 128
RET_THETA = 10000.0
H_B, KV_B, HD_B = 8, 2, 64
G_B = H_B // KV_B
WINDOW = 128
ROPE_THETA = 500000.0
ROT_B = HD_B // 4
W_C = 256
POOL_WINDOWS = (2, 4, 8, 16)
GC = 64
POOL_HIST = 15
H_D, DK_D, DV_D = 4, 64, 64
HGRN_CHUNK = 16
W_A, W_B, W_D = H_A * DV_A, H_B * HD_B, H_D * DV_D
BRANCH_WIDTHS = (W_A, W_B, W_C, W_D)
W_MIX = sum(BRANCH_WIDTHS)
EPS = 1e-6

QA, KA, VA, ZA = 0, 128, 256, 512
QB, KB, VB, ZB = 768, 1280, 1408, 1536
UC, ZC = 2048, 2304
QD, FD, ID, ZD = 2560, 2816, 3072, 3328
GL = 3584
D_IN = GL + 4 * D_MODEL
Y_OFF = (0, W_A, W_A + W_B, W_A + W_B + W_C)

LANES = 128
VMEM_LIMIT = 56 * 1024 * 1024
PROMPT_TILE = 256


def _dot(a, b):
    return jnp.dot(a.astype(BF16), b.astype(BF16), preferred_element_type=F32)


def _dot_nt(a, b):
    return lax.dot_general(a.astype(BF16), b.astype(BF16), (((1,), (1,)), ((), ())),
                           preferred_element_type=F32)


def _dot_tn(a, b):
    return lax.dot_general(a.astype(BF16), b.astype(BF16), (((0,), (0,)), ((), ())),
                           preferred_element_type=F32)


def _sigmoid(x):
    return 1.0 / (1.0 + jnp.exp(-x))


def _silu(x):
    return x * _sigmoid(x)


def _rms(x):
    return x * lax.rsqrt(jnp.mean(x * x, axis=-1, keepdims=True) + EPS)


def _shift_rows(x, j):
    m = x.shape[0]
    if m % 8 == 0:
        return pltpu.roll(x, j, 0)
    return jnp.concatenate([jnp.zeros((j, x.shape[1]), x.dtype), x[:m - j]], axis=0)


def _rope(x, cos, sin, half, group):
    n = x.shape[1]
    lane = lax.broadcasted_iota(jnp.int32, x.shape, 1)
    first = (lane % group) < half
    swapped = jnp.where(first, pltpu.roll(x, n - half, 1), pltpu.roll(x, half, 1))
    return x * cos + swapped * sin


def _group_mean_sq(o, ones_bd):
    sq = o * o
    hi = sq.astype(BF16)
    lo = (sq - hi.astype(F32)).astype(BF16)
    tot = (jnp.dot(hi, ones_bd, preferred_element_type=F32)
           + jnp.dot(lo, ones_bd, preferred_element_type=F32))
    return tot * (1.0 / 64.0)


def _retention_chunk(q, k, v, s_bd, t):
    kexp = jnp.concatenate([k] * H_A, axis=0) * t["hmask"]
    sc = _dot_nt(q, kexp) * t["dcat"]
    vexp = jnp.concatenate([v] * H_A, axis=0) * t["vmask"]
    lhs = jnp.concatenate([q * t["qdec"], sc], axis=1)
    rhs = jnp.concatenate([s_bd, vexp], axis=0)
    o = _dot(lhs, rhs)
    s_new = t["cdec"] * s_bd + t["bd_a"] * _dot_tn(k * t["kdec"], v)
    return o, s_new


def _swa_block(q, k_new, v_new, k_prev, v_prev, sink_col, valid):
    bq = q.shape[0]
    bias = jnp.where(valid, 0.0, -jnp.inf).astype(F32)
    bias = jnp.concatenate([bias] * G_B, axis=0)
    outs = []
    for kv in range(KV_B):
        lo = kv * HD_B
        qs = jnp.concatenate(
            [q[:, (kv * G_B + g) * HD_B:(kv * G_B + g + 1) * HD_B] for g in range(G_B)], axis=0)
        qs = qs * (HD_B ** -0.5)
        keys = jnp.concatenate([k_prev[:, lo:lo + HD_B], k_new[:, lo:lo + HD_B]], axis=0)
        vals = jnp.concatenate([v_prev[:, lo:lo + HD_B], v_new[:, lo:lo + HD_B]], axis=0)
        s = _dot_nt(qs, keys) + bias
        sk = sink_col[kv]
        m = jnp.maximum(jnp.max(s, axis=1, keepdims=True), sk)
        e = jnp.exp(s - m)
        vext = jnp.concatenate([vals, jnp.ones_like(vals)], axis=1)
        oe = _dot(e, vext)
        den = oe[:, HD_B:] + jnp.exp(sk - m)
        o = oe[:, :HD_B] / den
        outs.append(jnp.concatenate([o[g * bq:(g + 1) * bq] for g in range(G_B)], axis=1))
    return jnp.concatenate(outs, axis=1)


def _pool_tile(ext, pos0, wpool_bd, pool_scale):
    m = ext.shape[0] - 16
    s2 = ext[1:] + ext[:-1]
    s4 = s2[2:] + s2[:-2]
    s8 = s4[4:] + s4[:-4]
    s16 = s8[8:] + s8[:-8]
    u = ext[16:]
    lane = lax.broadcasted_iota(jnp.int32, (m, W_C), 1)
    grp = lane // GC
    win = jnp.where(grp == 0, s2[15:15 + m],
                    jnp.where(grp == 1, s4[13:13 + m],
                              jnp.where(grp == 2, s8[9:9 + m], s16[1:1 + m])))
    wlen = jnp.where(grp == 0, 2, jnp.where(grp == 1, 4, jnp.where(grp == 2, 8, 16)))
    pos = pos0 + lax.broadcasted_iota(jnp.int32, (m, W_C), 0)
    cnt = jnp.minimum(pos + 1, wlen).astype(F32)
    pooled = win / cnt - u
    return _dot(pooled, wpool_bd) * pool_scale


def _hgrn_tile(qd, fd, v, lb, st_bd, chunk, ones_bd, bd_d):
    m = qd.shape[0]
    f = lb + (1.0 - lb) * _sigmoid(fd)
    q = _silu(qd)
    k = 1.0 - f
    lf = jnp.log(f)
    rc = lax.broadcasted_iota(jnp.int32, (m, W_D), 0) % chunk
    b = lf
    sh = 1
    while sh < chunk:
        b = b + jnp.where(rc >= sh, _shift_rows(b, sh), 0.0)
        sh *= 2
    o = _dot(q * k, ones_bd) * v
    for j in range(1, chunk):
        a = jnp.where(rc >= j, q * _shift_rows(k, j) * jnp.exp(b - _shift_rows(b, j)), 0.0)
        o = o + _dot(a, ones_bd) * _shift_rows(v, j)
    qdec = q * jnp.exp(b)
    parts = []
    for n in range(m // chunk):
        r0 = n * chunk
        bl = b[r0 + chunk - 1:r0 + chunk]
        parts.append(_dot_nt(qdec[r0:r0 + chunk], st_bd))
        kd = k[r0:r0 + chunk] * jnp.exp(bl - b[r0:r0 + chunk])
        st_bd = jnp.exp(bl) * st_bd + bd_d * _dot_tn(v[r0:r0 + chunk], kd)
    o = o + (parts[0] if len(parts) == 1 else jnp.concatenate(parts, axis=0))
    return o, st_bd


def _lower_bound_row(lower_bounds, layer):
    x = lower_bounds
    mx = jnp.max(x, axis=0, keepdims=True)
    e = jnp.exp(x - mx)
    tot = jnp.sum(e, axis=0, keepdims=True)
    if layer == 0:
        return jnp.zeros_like(tot)
    acc = e[1:2]
    for i in range(2, layer + 1):
        acc = acc + e[i:i + 1]
    return acc / tot


def _swa_valid(bq, lim):
    row = lax.broadcasted_iota(jnp.int32, (bq, WINDOW + bq), 0)
    col = lax.broadcasted_iota(jnp.int32, (bq, WINDOW + bq), 1)
    ok = jnp.where(col >= row, jnp.where(col <= row + WINDOW, 1, 0), 0)
    ok = jnp.where(col >= lim, ok, 0)
    return ok > 0


def _rope_tables(pos):
    pos = pos.astype(F32)[:, None]
    half_a = DK_A // 2
    inv_a = jnp.power(RET_THETA, -jnp.arange(half_a, dtype=F32) / half_a)
    ang = pos * inv_a[None, :]
    cos_a = jnp.tile(jnp.concatenate([jnp.cos(ang), jnp.cos(ang)], -1), (1, H_A))
    sin_a = jnp.tile(jnp.concatenate([-jnp.sin(ang), jnp.sin(ang)], -1), (1, H_A))
    half_b = ROT_B // 2
    inv_b = jnp.power(ROPE_THETA, -jnp.arange(half_b, dtype=F32) / half_b)
    angb = pos * inv_b[None, :]
    t = pos.shape[0]
    one = jnp.ones((t, HD_B - ROT_B), F32)
    zero = jnp.zeros((t, HD_B - ROT_B), F32)
    cos_b = jnp.tile(jnp.concatenate([jnp.cos(angb), jnp.cos(angb), one], -1), (1, 2))
    sin_b = jnp.tile(jnp.concatenate([-jnp.sin(angb), jnp.sin(angb), zero], -1), (1, 2))
    return cos_a, sin_a, cos_b, sin_b


def _ret_tables(c):
    lg = jnp.log1p(-jnp.power(2.0, -5.0 - jnp.arange(H_A, dtype=F32)))
    idx = jnp.arange(c, dtype=F32)
    rel = idx[:, None] - idx[None, :]
    dmask = jnp.where(rel[None] >= 0, jnp.exp(jnp.maximum(rel, 0.0)[None] * lg[:, None, None]), 0.0)
    dcat = jnp.transpose(dmask, (1, 0, 2)).reshape(c, H_A * c)
    head_of_row = jnp.repeat(jnp.arange(H_A), c)
    hmask = (head_of_row[:, None] == (jnp.arange(128) // DK_A)[None, :]).astype(F32)
    vmask = (head_of_row[:, None] == (jnp.arange(W_A) // DV_A)[None, :]).astype(F32)
    lg_lane = jnp.repeat(lg, DK_A)
    qdec = jnp.exp((idx[:, None] + 1.0) * lg_lane[None, :])
    kdec = jnp.exp((c - 1.0 - idx)[:, None] * lg_lane[None, :])
    bd_a = ((jnp.arange(128) // DK_A)[:, None] == (jnp.arange(W_A) // DV_A)[None, :]).astype(F32)
    cdec = jnp.exp(c * lg_lane)[:, None] * bd_a
    return dict(hmask=hmask, dcat=dcat, vmask=vmask, qdec=qdec, kdec=kdec, cdec=cdec, bd_a=bd_a)


RET_TABLE_NAMES = ("hmask", "dcat", "vmask", "qdec", "kdec", "cdec", "bd_a")


def _block_ones_64():
    g = jnp.arange(256) // 64
    return (g[:, None] == g[None, :])


def _pool_weight_bd(w_pool_l):
    out = jnp.zeros((W_C, W_C), F32)
    for g in range(len(POOL_WINDOWS)):
        out = out.at[g * GC:(g + 1) * GC, g * GC:(g + 1) * GC].set(w_pool_l[g])
    return out.astype(BF16)


def _prompt_layer_body(layer, tt, *refs):
    (x_ref, cosa_ref, sina_ref, cosb_ref, sinb_ref,
     hmask_ref, dcat_ref, vmask_ref, qdec_ref, kdec_ref, cdec_ref, bda_ref,
     ones_ref, bdd_ref, win_ref, wbr_ref, wout_ref, gpre_ref, gpost_ref, sink_ref,
     wpool_ref, pscale_ref, ghgrn_ref, lbnd_ref,
     xo_ref, ret_ref, ko_ref, vo_ref, poolo_ref, hgo_ref,
     hb_ref, y_ref, sbd_ref, stbd_ref, kprev_ref, vprev_ref, ext_ref) = refs
    t_i = pl.program_id(1)
    n_t = pl.num_programs(1)

    @pl.when(t_i == 0)
    def _():
        sbd_ref[...] = jnp.zeros_like(sbd_ref)
        stbd_ref[...] = jnp.zeros_like(stbd_ref)
        kprev_ref[...] = jnp.zeros_like(kprev_ref)
        vprev_ref[...] = jnp.zeros_like(vprev_ref)
        ext_ref[0:16, :] = jnp.zeros((16, W_C), F32)

    x = x_ref[0]
    hb_ref[...] = (_rms(x) * gpre_ref[...]).astype(BF16)
    ones_bd = ones_ref[...]
    rt = dict(hmask=hmask_ref[...], dcat=dcat_ref[...], vmask=vmask_ref[...], qdec=qdec_ref[...],
              kdec=kdec_ref[...], cdec=cdec_ref[...], bd_a=bda_ref[...])
    nblk = tt // RET_CHUNK

    pa = jnp.dot(hb_ref[...], win_ref[:, QA:ZA + W_A], preferred_element_type=F32)
    qa = _rope(pa[:, QA:QA + 128], cosa_ref[...], sina_ref[...], DK_A // 2, DK_A)
    ka = _rope(pa[:, KA:KA + 128], cosa_ref[...], sina_ref[...], DK_A // 2, DK_A) * (DK_A ** -0.5)
    va = pa[:, VA:VA + W_A]
    s_bd = sbd_ref[...]
    oa = []
    for c in range(nblk):
        r = slice(c * RET_CHUNK, (c + 1) * RET_CHUNK)
        o_c, s_bd = _retention_chunk(qa[r], ka[r], va[r], s_bd, rt)
        oa.append(o_c)
    sbd_ref[...] = s_bd
    oa = jnp.concatenate(oa, axis=0)
    oa = oa * lax.rsqrt(_group_mean_sq(oa, ones_bd) + EPS)
    y_ref[:, Y_OFF[0]:Y_OFF[0] + W_A] = (oa * _silu(pa[:, ZA:ZA + W_A])).astype(BF16)

    pb = jnp.dot(hb_ref[...], win_ref[:, QB:ZB + W_B], preferred_element_type=F32)
    cosb, sinb = cosb_ref[...], sinb_ref[...]
    qb = _rope(pb[:, 0:W_B], jnp.concatenate([cosb] * 4, axis=1), jnp.concatenate([sinb] * 4, axis=1),
               ROT_B // 2, HD_B)
    kb = _rope(pb[:, KB - QB:KB - QB + 128], cosb, sinb, ROT_B // 2, HD_B)
    vb = pb[:, VB - QB:VB - QB + 128]
    sink_col = sink_ref[...]
    ob = []
    for c in range(nblk):
        r = slice(c * WINDOW, (c + 1) * WINDOW)
        lim = jnp.where(t_i == 0, WINDOW, 0) if c == 0 else 0
        valid = _swa_valid(WINDOW, lim)
        ob.append(_swa_block(qb[r], kb[r], vb[r], kprev_ref[...], vprev_ref[...], sink_col, valid))
        kprev_ref[...] = kb[r]
        vprev_ref[...] = vb[r]
    ob = jnp.concatenate(ob, axis=0)
    y_ref[:, Y_OFF[1]:Y_OFF[1] + W_B] = (ob * _silu(pb[:, ZB - QB:ZB - QB + W_B])).astype(BF16)

    pc = jnp.dot(hb_ref[...], win_ref[:, UC:ZC + W_C], preferred_element_type=F32)
    ext_ref[16:16 + tt, :] = pc[:, 0:W_C]
    oc = _pool_tile(ext_ref[...], t_i * tt, wpool_ref[...], pscale_ref[...])
    y_ref[:, Y_OFF[2]:Y_OFF[2] + W_C] = (oc * _silu(pc[:, W_C:2 * W_C])).astype(BF16)

    @pl.when(t_i == n_t - 1)
    def _():
        poolo_ref[0] = ext_ref[pl.ds(tt + 1, POOL_HIST), :]

    ext_ref[0:16, :] = ext_ref[tt:tt + 16, :]

    pd = jnp.dot(hb_ref[...], win_ref[:, QD:ZD + W_D], preferred_element_type=F32)
    lb = _lower_bound_row(lbnd_ref[...], layer)
    od, st_bd = _hgrn_tile(pd[:, 0:256], pd[:, 256:512], pd[:, 512:768], lb, stbd_ref[...],
                           HGRN_CHUNK, ones_bd, bdd_ref[...])
    stbd_ref[...] = st_bd
    od = od * lax.rsqrt(_group_mean_sq(od, ones_bd) + EPS) * ghgrn_ref[...]
    y_ref[:, Y_OFF[3]:Y_OFF[3] + W_D] = (od * _silu(pd[:, 768:1024])).astype(BF16)

    merged = None
    for i in range(4):
        g = _sigmoid(jnp.dot(hb_ref[...], win_ref[:, GL + i * D_MODEL:GL + (i + 1) * D_MODEL],
                             preferred_element_type=F32))
        term = g * jnp.dot(y_ref[:, Y_OFF[i]:Y_OFF[i] + BRANCH_WIDTHS[i]],
                           wbr_ref[Y_OFF[i]:Y_OFF[i] + BRANCH_WIDTHS[i], :], preferred_element_type=F32)
        merged = term if merged is None else merged + term
    out = jnp.dot(merged.astype(BF16), wout_ref[...], preferred_element_type=F32)
    xo_ref[0] = x + _rms(out) * gpost_ref[...]

    @pl.when(t_i == n_t - 1)
    def _():
        s = sbd_ref[...]
        ret_ref[0] = s[:, 0:64] + s[:, 64:128] + s[:, 128:192] + s[:, 192:256]
        st = stbd_ref[...]
        hgo_ref[0] = (st[0:64] + st[64:128] + st[128:192] + st[192:256]).T
        ko_ref[0] = kprev_ref[...]
        vo_ref[0] = vprev_ref[...]


def _const_spec(shape):
    nd = len(shape)
    return pl.BlockSpec(shape, lambda b, t: (0,) * nd, pipeline_mode=pl.Buffered(1))


def _prompt_layer(layer, x, tabs, consts, w_in, w_br, w_out, g_pre, g_post, sink_col, wpool_bd, pscale,
                  g_hgrn, lower_bounds):
    bsz, seq, _ = x.shape
    tt = PROMPT_TILE
    nt = seq // tt
    cos_a, sin_a, cos_b, sin_b = tabs
    rt, ones_bd, bd_d = consts
    tab_spec = pl.BlockSpec((tt, LANES), lambda b, t: (t, 0))
    in_arrays = [x, cos_a, sin_a, cos_b, sin_b] + [rt[n] for n in RET_TABLE_NAMES] + [
        ones_bd, bd_d, w_in, w_br, w_out, g_pre, g_post, sink_col, wpool_bd, pscale, g_hgrn, lower_bounds]
    in_specs = [pl.BlockSpec((1, tt, D_MODEL), lambda b, t: (b, t, 0)), tab_spec, tab_spec, tab_spec, tab_spec]
    in_specs += [_const_spec(a.shape) for a in in_arrays[5:]]
    out_shape = (
        jax.ShapeDtypeStruct((bsz, seq, D_MODEL), F32),
        jax.ShapeDtypeStruct((bsz, H_A * DK_A, DV_A), F32),
        jax.ShapeDtypeStruct((bsz, WINDOW, KV_B * HD_B), F32),
        jax.ShapeDtypeStruct((bsz, WINDOW, KV_B * HD_B), F32),
        jax.ShapeDtypeStruct((bsz, POOL_HIST, W_C), F32),
        jax.ShapeDtypeStruct((bsz, H_D * DK_D, DV_D), F32),
    )
    out_specs = (
        pl.BlockSpec((1, tt, D_MODEL), lambda b, t: (b, t, 0)),
        pl.BlockSpec((1, H_A * DK_A, DV_A), lambda b, t: (b, 0, 0)),
        pl.BlockSpec((1, WINDOW, KV_B * HD_B), lambda b, t: (b, 0, 0)),
        pl.BlockSpec((1, WINDOW, KV_B * HD_B), lambda b, t: (b, 0, 0)),
        pl.BlockSpec((1, POOL_HIST, W_C), lambda b, t: (b, 0, 0)),
        pl.BlockSpec((1, H_D * DK_D, DV_D), lambda b, t: (b, 0, 0)),
    )
    scratch = [
        pltpu.VMEM((tt, D_MODEL), BF16),
        pltpu.VMEM((tt, W_MIX), BF16),
        pltpu.VMEM((H_A * DK_A, W_A), F32),
        pltpu.VMEM((W_D, H_D * DK_D), F32),
        pltpu.VMEM((WINDOW, KV_B * HD_B), F32),
        pltpu.VMEM((WINDOW, KV_B * HD_B), F32),
        pltpu.VMEM((16 + tt, W_C), F32),
    ]
    return pl.pallas_call(
        functools.partial(_prompt_layer_body, layer, tt),
        grid=(bsz, nt),
        in_specs=in_specs,
        out_specs=out_specs,
        out_shape=out_shape,
        scratch_shapes=scratch,
        compiler_params=pltpu.CompilerParams(
            dimension_semantics=("arbitrary", "arbitrary"), vmem_limit_bytes=VMEM_LIMIT),
        name=f"prompt_layer{layer}",
    )(*in_arrays)


def _sample_inproj_body(x_ref, gpre_ref, w_ref, o_ref):
    hb = (_rms(x_ref[...]) * gpre_ref[...]).astype(BF16)
    o_ref[...] = jnp.dot(hb, w_ref[...], preferred_element_type=F32)


def _sample_inproj(layer, x2, g_pre, w_in):
    n = x2.shape[0]
    cb = 512
    return pl.pallas_call(
        _sample_inproj_body,
        grid=(D_IN // cb,),
        in_specs=[pl.BlockSpec((n, D_MODEL), lambda j: (0, 0)),
                  pl.BlockSpec((1, D_MODEL), lambda j: (0, 0)),
                  pl.BlockSpec((D_MODEL, cb), lambda j: (0, j))],
        out_specs=pl.BlockSpec((n, cb), lambda j: (0, j)),
        out_shape=jax.ShapeDtypeStruct((n, D_IN), F32),
        compiler_params=pltpu.CompilerParams(dimension_semantics=("arbitrary",), vmem_limit_bytes=VMEM_LIMIT),
        name=f"sample_inproj{layer}",
    )(x2, g_pre, w_in)


def _sample_mixer_body(layer, tq, *refs):
    (p_ref, cosa_ref, sina_ref, cosb_ref, sinb_ref,
     hmask_ref, dcat_ref, vmask_ref, qdec_ref, kdec_ref, cdec_ref, bda_ref,
     ones_ref, bdd_ref, sink_ref, wpool_ref, pscale_ref, ghgrn_ref, lbnd_ref,
     sret_ref, ck_ref, cv_ref, hist_ref, shg_ref,
     y_ref, reto_ref, ko_ref, vo_ref, poolo_ref, hgo_ref) = refs
    p = p_ref[0]
    ones_bd = ones_ref[...]
    rt = dict(hmask=hmask_ref[...], dcat=dcat_ref[...], vmask=vmask_ref[...], qdec=qdec_ref[...],
              kdec=kdec_ref[...], cdec=cdec_ref[...], bd_a=bda_ref[...])

    qa = _rope(p[:, QA:QA + 128], cosa_ref[...], sina_ref[...], DK_A // 2, DK_A)
    ka = _rope(p[:, KA:KA + 128], cosa_ref[...], sina_ref[...], DK_A // 2, DK_A) * (DK_A ** -0.5)
    va = p[:, VA:VA + W_A]
    s_bd = jnp.concatenate([sret_ref[0]] * H_A, axis=1) * rt["bd_a"]
    oa, s_bd = _retention_chunk(qa, ka, va, s_bd, rt)
    reto_ref[0] = s_bd[:, 0:64] + s_bd[:, 64:128] + s_bd[:, 128:192] + s_bd[:, 192:256]
    oa = oa * lax.rsqrt(_group_mean_sq(oa, ones_bd) + EPS)
    ya = oa * _silu(p[:, ZA:ZA + W_A])

    cosb, sinb = cosb_ref[...], sinb_ref[...]
    qb = _rope(p[:, QB:QB + W_B], jnp.concatenate([cosb] * 4, axis=1), jnp.concatenate([sinb] * 4, axis=1),
               ROT_B // 2, HD_B)
    kb = _rope(p[:, KB:KB + 128], cosb, sinb, ROT_B // 2, HD_B)
    vb = p[:, VB:VB + 128]
    k_prev, v_prev = ck_ref[0], cv_ref[0]
    ob = _swa_block(qb, kb, vb, k_prev, v_prev, sink_ref[...], _swa_valid(tq, 0))
    ko_ref[0] = jnp.concatenate([k_prev[tq:], kb], axis=0)
    vo_ref[0] = jnp.concatenate([v_prev[tq:], vb], axis=0)
    yb = ob * _silu(p[:, ZB:ZB + W_B])

    u = p[:, UC:UC + W_C]
    ext = jnp.concatenate([jnp.zeros((1, W_C), F32), hist_ref[0], u], axis=0)
    oc = _pool_tile(ext, PAST_LEN, wpool_ref[...], pscale_ref[...])
    poolo_ref[0] = ext[tq + 1:tq + 16]
    yc = oc * _silu(p[:, ZC:ZC + W_C])

    lb = _lower_bound_row(lbnd_ref[...], layer)
    bd_d = bdd_ref[...]
    st = shg_ref[0].T
    st_bd = jnp.concatenate([st] * H_D, axis=0) * bd_d
    od, st_bd = _hgrn_tile(p[:, QD:QD + 256], p[:, FD:FD + 256], p[:, ID:ID + 256], lb, st_bd,
                           math.gcd(tq, HGRN_CHUNK), ones_bd, bd_d)
    hgo_ref[0] = (st_bd[0:64] + st_bd[64:128] + st_bd[128:192] + st_bd[192:256]).T
    od = od * lax.rsqrt(_group_mean_sq(od, ones_bd) + EPS) * ghgrn_ref[...]
    yd = od * _silu(p[:, ZD:ZD + W_D])

    y_ref[0] = jnp.concatenate([ya, yb, yc, yd], axis=1)


def _sample_mixer(layer, proj3, tabs, consts, sink_col, wpool_bd, pscale, g_hgrn, lower_bounds,
                  s_ret, c_k, c_v, hist, s_hg):
    nb, tq, _ = proj3.shape
    cos_a, sin_a, cos_b, sin_b = tabs
    rt, ones_bd, bd_d = consts
    const_arrays = [cos_a, sin_a, cos_b, sin_b] + [rt[n] for n in RET_TABLE_NAMES] + [
        ones_bd, bd_d, sink_col, wpool_bd, pscale, g_hgrn, lower_bounds]

    def cspec(a):
        nd = a.ndim
        return pl.BlockSpec(a.shape, lambda b: (0,) * nd)

    def bspec(a):
        nd = a.ndim
        return pl.BlockSpec((1,) + a.shape[1:], lambda b: (b,) + (0,) * (nd - 1))

    state_arrays = [s_ret, c_k, c_v, hist, s_hg]
    out_shape = (jax.ShapeDtypeStruct((nb, tq, W_MIX), F32),) + tuple(
        jax.ShapeDtypeStruct(a.shape, F32) for a in state_arrays)
    return pl.pallas_call(
        functools.partial(_sample_mixer_body, layer, tq),
        grid=(nb,),
        in_specs=[bspec(proj3)] + [cspec(a) for a in const_arrays] + [bspec(a) for a in state_arrays],
        out_specs=tuple(bspec(s) for s in out_shape),
        out_shape=out_shape,
        compiler_params=pltpu.CompilerParams(dimension_semantics=("arbitrary",), vmem_limit_bytes=VMEM_LIMIT),
        name=f"sample_mixer{layer}",
    )(proj3, *const_arrays, *state_arrays)


def _sample_merge_body(x_ref, y_ref, gl_ref, wbr_ref, wout_ref, gpost_ref, o_ref):
    merged = None
    for i in range(4):
        g = _sigmoid(gl_ref[:, i * D_MODEL:(i + 1) * D_MODEL])
        term = g * jnp.dot(y_ref[:, Y_OFF[i]:Y_OFF[i] + BRANCH_WIDTHS[i]].astype(BF16),
                           wbr_ref[Y_OFF[i]:Y_OFF[i] + BRANCH_WIDTHS[i], :], preferred_element_type=F32)
        merged = term if merged is None else merged + term
    out = jnp.dot(merged.astype(BF16), wout_ref[...], preferred_element_type=F32)
    o_ref[...] = x_ref[...] + _rms(out) * gpost_ref[...]


def _sample_merge(layer, x2, y2, gl, w_br, w_out, g_post):
    return pl.pallas_call(
        _sample_merge_body,
        out_shape=jax.ShapeDtypeStruct(x2.shape, F32),
        compiler_params=pltpu.CompilerParams(vmem_limit_bytes=VMEM_LIMIT),
        name=f"sample_merge{layer}",
    )(x2, y2, gl, w_br, w_out, g_post)


def kernel(x_prompt, x_sample, state_ret, cache_swa_k, cache_swa_v, state_pool, state_hgrn, w_in, w_branch,
           w_out, g_pre, g_post, attn_sink, w_pool, pool_scale, g_hgrn, lower_bounds):
    bsz, seq, _ = x_prompt.shape
    nb, tq, _ = x_sample.shape
    w_in_b = w_in.astype(BF16)
    w_br_b = w_branch.astype(BF16)
    w_out_b = w_out.astype(BF16)
    ones_bd = _block_ones_64().astype(BF16)
    bd_d = _block_ones_64().astype(F32)
    consts_p = (_ret_tables(RET_CHUNK), ones_bd, bd_d)
    consts_s = (_ret_tables(math.gcd(tq, RET_CHUNK)), ones_bd, bd_d)
    tabs_p = _rope_tables(jnp.arange(seq))
    tabs_s = _rope_tables(PAST_LEN + jnp.arange(tq))

    def sink_cols(sink_l, rows):
        return jnp.repeat(sink_l.reshape(KV_B, G_B), rows, axis=1)[..., None].astype(F32)

    xp = x_prompt
    xs = x_sample.reshape(nb * tq, D_MODEL)
    acc_p = [[] for _ in range(5)]
    acc_s = [[] for _ in range(5)]
    for l in range(DEPTH):
        wpool_bd = _pool_weight_bd(w_pool[l])
        pscale = pool_scale[l][None, :]
        ghg = g_hgrn[l][None, :]
        gpre = g_pre[l][None, :]
        gpost = g_post[l][None, :]

        xp, r_p, k_p, v_p, pool_p, h_p = _prompt_layer(
            l, xp, tabs_p, consts_p, w_in_b[l], w_br_b[l], w_out_b[l], gpre, gpost,
            sink_cols(attn_sink[l], WINDOW), wpool_bd, pscale, ghg, lower_bounds)
        for a, s in zip(acc_p, (r_p.reshape(bsz, H_A, DK_A, DV_A), k_p.reshape(bsz, WINDOW, KV_B, HD_B),
                                v_p.reshape(bsz, WINDOW, KV_B, HD_B), pool_p,
                                h_p.reshape(bsz, H_D, DK_D, DV_D))):
            a.append(s)

        proj = _sample_inproj(l, xs, gpre, w_in_b[l])
        y3, r_s, k_s, v_s, pool_s, h_s = _sample_mixer(
            l, proj[:, :GL].reshape(nb, tq, GL), tabs_s, consts_s, sink_cols(attn_sink[l], tq), wpool_bd,
            pscale, ghg, lower_bounds,
            state_ret[l].reshape(nb, H_A * DK_A, DV_A), cache_swa_k[l].reshape(nb, WINDOW, KV_B * HD_B),
            cache_swa_v[l].reshape(nb, WINDOW, KV_B * HD_B), state_pool[l],
            state_hgrn[l].reshape(nb, H_D * DK_D, DV_D))
        xs = _sample_merge(l, xs, y3.reshape(nb * tq, W_MIX), proj[:, GL:], w_br_b[l], w_out_b[l], gpost)
        for a, s in zip(acc_s, (r_s.reshape(nb, H_A, DK_A, DV_A), k_s.reshape(nb, WINDOW, KV_B, HD_B),
                                v_s.reshape(nb, WINDOW, KV_B, HD_B), pool_s,
                                h_s.reshape(nb, H_D, DK_D, DV_D))):
            a.append(s)

    outs_p = [jnp.stack(a) for a in acc_p]
    outs_s = [jnp.stack(a) for a in acc_s]
    return (xp, xs.reshape(nb, tq, D_MODEL), *outs_p, *outs_s)
```

```python
import functools
import math

import numpy as np
import jax
import jax.numpy as jnp
from jax import lax
from jax.experimental import pallas as pl
from jax.experimental.pallas import tpu as pltpu

F32 = jnp.float32
BF16 = jnp.bfloat16

D_MODEL = 1024
DEPTH = 4
PAST_LEN = 8192
H_A, DK_A, DV_A = 4, 32, 64
RET_CHUNK = 128
RET_THETA = 10000.0
H_B, KV_B, HD_B = 8, 2, 64
G_B = H_B // KV_B
WINDOW = 128
ROPE_THETA = 500000.0
ROT_B = HD_B // 4
W_C = 256
POOL_WINDOWS = (2, 4, 8, 16)
GC = 64
POOL_HIST = 15
H_D, DK_D, DV_D = 4, 64, 64
HGRN_CHUNK = 16
W_A, W_B, W_D = H_A * DV_A, H_B * HD_B, H_D * DV_D
BRANCH_WIDTHS = (W_A, W_B, W_C, W_D)
W_MIX = sum(BRANCH_WIDTHS)
EPS = 1e-6

QA, KA, VA, ZA = 0, 128, 256, 512
QB, KB, VB, ZB = 768, 1280, 1408, 1536
UC, ZC = 2048, 2304
QD, FD, ID, ZD = 2560, 2816, 3072, 3328
GL = 3584
D_IN = GL + 4 * D_MODEL
Y_OFF = (0, W_A, W_A + W_B, W_A + W_B + W_C)

LANES = 128
VMEM_LIMIT = 56 * 1024 * 1024
PROMPT_TILE = 256


def _dot(a, b):
    return jnp.dot(a.astype(BF16), b.astype(BF16), preferred_element_type=F32)


def _dot_nt(a, b):
    return lax.dot_general(a.astype(BF16), b.astype(BF16), (((1,), (1,)), ((), ())),
                           preferred_element_type=F32)


def _dot_tn(a, b):
    return lax.dot_general(a.astype(BF16), b.astype(BF16), (((0,), (0,)), ((), ())),
                           preferred_element_type=F32)


def _sigmoid(x):
    return 1.0 / (1.0 + jnp.exp(-x))


def _silu(x):
    return x * _sigmoid(x)


def _rms(x):
    return x * lax.rsqrt(jnp.mean(x * x, axis=-1, keepdims=True) + EPS)


def _shift_rows(x, j):
    m = x.shape[0]
    if m % 8 == 0:
        return pltpu.roll(x, j, 0)
    return jnp.concatenate([jnp.zeros((j, x.shape[1]), x.dtype), x[:m - j]], axis=0)


def _rope(x, cos, sin, half, group):
    n = x.shape[1]
    lane = lax.broadcasted_iota(jnp.int32, x.shape, 1)
    first = (lane % group) < half
    swapped = jnp.where(first, pltpu.roll(x, n - half, 1), pltpu.roll(x, half, 1))
    return x * cos + swapped * sin


def _group_mean_sq(o, ones_bd):
    sq = o * o
    hi = sq.astype(BF16)
    lo = (sq - hi.astype(F32)).astype(BF16)
    tot = (jnp.dot(hi, ones_bd, preferred_element_type=F32)
           + jnp.dot(lo, ones_bd, preferred_element_type=F32))
    return tot * (1.0 / 64.0)


def _retention_chunk(q, k, v, s_bd, t):
    kexp = jnp.concatenate([k] * H_A, axis=0) * t["hmask"]
    sc = _dot_nt(q, kexp) * t["dcat"]
    vexp = jnp.concatenate([v] * H_A, axis=0) * t["vmask"]
    lhs = jnp.concatenate([q * t["qdec"], sc], axis=1)
    rhs = jnp.concatenate([s_bd, vexp], axis=0)
    o = _dot(lhs, rhs)
    s_new = t["cdec"] * s_bd + t["bd_a"] * _dot_tn(k * t["kdec"], v)
    return o, s_new


def _swa_block(q, k_new, v_new, k_prev, v_prev, sink_col, valid):
    bq = q.shape[0]
    bias = jnp.where(valid, 0.0, -jnp.inf).astype(F32)
    bias = jnp.concatenate([bias] * G_B, axis=0)
    outs = []
    for kv in range(KV_B):
        lo = kv * HD_B
        qs = jnp.concatenate(
            [q[:, (kv * G_B + g) * HD_B:(kv * G_B + g + 1) * HD_B] for g in range(G_B)], axis=0)
        qs = qs * (HD_B ** -0.5)
        keys = jnp.concatenate([k_prev[:, lo:lo + HD_B], k_new[:, lo:lo + HD_B]], axis=0)
        vals = jnp.concatenate([v_prev[:, lo:lo + HD_B], v_new[:, lo:lo + HD_B]], axis=0)
        s = _dot_nt(qs, keys) + bias
        sk = sink_col[kv]
        m = jnp.maximum(jnp.max(s, axis=1, keepdims=True), sk)
        e = jnp.exp(s - m)
        vext = jnp.concatenate([vals, jnp.ones_like(vals)], axis=1)
        oe = _dot(e, vext)
        den = oe[:, HD_B:] + jnp.exp(sk - m)
        o = oe[:, :HD_B] / den
        outs.append(jnp.concatenate([o[g * bq:(g + 1) * bq] for g in range(G_B)], axis=1))
    return jnp.concatenate(outs, axis=1)


def _pool_tile(ext, pos0, wpool_bd, pool_scale):
    m = ext.shape[0] - 16
    s2 = ext[1:] + ext[:-1]
    s4 = s2[2:] + s2[:-2]
    s8 = s4[4:] + s4[:-4]
    s16 = s8[8:] + s8[:-8]
    u = ext[16:]
    lane = lax.broadcasted_iota(jnp.int32, (m, W_C), 1)
    grp = lane // GC
    win = jnp.where(grp == 0, s2[15:15 + m],
                    jnp.where(grp == 1, s4[13:13 + m],
                              jnp.where(grp == 2, s8[9:9 + m], s16[1:1 + m])))
    wlen = jnp.where(grp == 0, 2, jnp.where(grp == 1, 4, jnp.where(grp == 2, 8, 16)))
    pos = pos0 + lax.broadcasted_iota(jnp.int32, (m, W_C), 0)
    cnt = jnp.minimum(pos + 1, wlen).astype(F32)
    pooled = win / cnt - u
    return _dot(pooled, wpool_bd) * pool_scale


def _hgrn_tile(qd, fd, v, lb, st_bd, chunk, ones_bd, bd_d):
    m = qd.shape[0]
    f = lb + (1.0 - lb) * _sigmoid(fd)
    q = _silu(qd)
    k = 1.0 - f
    lf = jnp.log(f)
    rc = lax.broadcasted_iota(jnp.int32, (m, W_D), 0) % chunk
    b = lf
    sh = 1
    while sh < chunk:
        b = b + jnp.where(rc >= sh, _shift_rows(b, sh), 0.0)
        sh *= 2
    o = _dot(q * k, ones_bd) * v
    for j in range(1, chunk):
        a = jnp.where(rc >= j, q * _shift_rows(k, j) * jnp.exp(b - _shift_rows(b, j)), 0.0)
        o = o + _dot(a, ones_bd) * _shift_rows(v, j)
    qdec = q * jnp.exp(b)
    parts = []
    for n in range(m // chunk):
        r0 = n * chunk
        bl = b[r0 + chunk - 1:r0 + chunk]
        parts.append(_dot_nt(qdec[r0:r0 + chunk], st_bd))
        kd = k[r0:r0 + chunk] * jnp.exp(bl - b[r0:r0 + chunk])
        st_bd = jnp.exp(bl) * st_bd + bd_d * _dot_tn(v[r0:r0 + chunk], kd)
    o = o + (parts[0] if len(parts) == 1 else jnp.concatenate(parts, axis=0))
    return o, st_bd


HG_BLOCK = 128
HG_LEVELS = 7


def _hgrn_block_tree(q, k, v, lf, st_bd, t, ones_bd, bd_d):
    n = HG_BLOCK
    hi = lf.astype(BF16)
    r1 = lf - hi.astype(F32)
    mid = r1.astype(BF16)
    lo = (r1 - mid.astype(F32)).astype(BF16)
    amat = t["hg_a"]
    ex = (jnp.dot(amat, hi, preferred_element_type=F32) + jnp.dot(amat, mid, preferred_element_type=F32)
          + jnp.dot(amat, lo, preferred_element_type=F32))
    e_all = jnp.exp(ex)
    qdec = q * e_all[0:n]
    kdec = k * e_all[n:2 * n]
    row = lax.broadcasted_iota(jnp.int32, (n, W_D), 0)
    lane = lax.broadcasted_iota(jnp.int32, (n, LANES), 1)
    lvl = t["hg_lvl"]
    xs = [jnp.zeros((n, n), F32) for _ in range(H_D)]
    for i in range(HG_LEVELS):
        s = 1 << i
        e_s = e_all[(2 + i) * n:(3 + i) * n]
        second = (row // s) % 2 == 1
        p = jnp.where(second, q, k) * e_s
        qt = jnp.where(second, p, 0.0).astype(BF16)
        kt = jnp.where(second, 0.0, p).astype(BF16)
        for h in range(H_D):
            c0 = (h // 2) * LANES
            in_head = (lane // DK_D) == (h % 2)
            qh = jnp.where(in_head, qt[:, c0:c0 + LANES], jnp.zeros((), BF16))
            x = lax.dot_general(qh, kt[:, c0:c0 + LANES], (((1,), (1,)), ((), ())),
                                preferred_element_type=F32)
            xs[h] = jnp.where(lvl == i, x, xs[h])
    vb = v.astype(BF16)
    vexp = jnp.concatenate([vb] * H_D, axis=0) * t["hg_vmask"]
    sc = jnp.concatenate(xs, axis=1).astype(BF16)
    o = jnp.dot(sc, vexp, preferred_element_type=F32)
    o = o + _dot(q * k, ones_bd) * v
    o = o + _dot_nt(qdec, st_bd)
    st_new = e_all[n - 1:n] * st_bd + bd_d * _dot_tn(v, kdec)
    return o, st_new


def _hgrn_tables():
    n = HG_BLOCK
    t = np.arange(n)[:, None]
    u = np.arange(n)[None, :]
    mats = [(u <= t), (u > t)]
    for i in range(HG_LEVELS):
        s = 1 << i
        midp = (t // (2 * s)) * 2 * s + s - 1
        second = (t % (2 * s)) >= s
        mats.append(np.where(second, (u > midp) & (u <= t), (u > t) & (u <= midp)))
    amat = np.concatenate(mats, axis=0).astype(np.float32)
    x = np.bitwise_xor(t, u)
    lvl = np.where(u < t, np.floor(np.log2(np.maximum(x, 1))).astype(np.int32), -1).astype(np.int32)
    head_of_row = np.repeat(np.arange(H_D), n)
    vmask = (head_of_row[:, None] == (np.arange(W_D) // DV_D)[None, :]).astype(np.float32)
    return dict(hg_a=jnp.asarray(amat, BF16), hg_lvl=jnp.asarray(lvl), hg_vmask=jnp.asarray(vmask, BF16))


HG_TABLE_NAMES = ("hg_a", "hg_lvl", "hg_vmask")


def _lower_bound_row(lower_bounds, layer):
    x = lower_bounds
    mx = jnp.max(x, axis=0, keepdims=True)
    e = jnp.exp(x - mx)
    tot = jnp.sum(e, axis=0, keepdims=True)
    if layer == 0:
        return jnp.zeros_like(tot)
    acc = e[1:2]
    for i in range(2, layer + 1):
        acc = acc + e[i:i + 1]
    return acc / tot


def _swa_valid(bq, lim):
    row = lax.broadcasted_iota(jnp.int32, (bq, WINDOW + bq), 0)
    col = lax.broadcasted_iota(jnp.int32, (bq, WINDOW + bq), 1)
    ok = jnp.where(col >= row, jnp.where(col <= row + WINDOW, 1, 0), 0)
    ok = jnp.where(col >= lim, ok, 0)
    return ok > 0


def _rope_tables(pos):
    pos = pos.astype(F32)[:, None]
    half_a = DK_A // 2
    inv_a = jnp.power(RET_THETA, -jnp.arange(half_a, dtype=F32) / half_a)
    ang = pos * inv_a[None, :]
    cos_a = jnp.tile(jnp.concatenate([jnp.cos(ang), jnp.cos(ang)], -1), (1, H_A))
    sin_a = jnp.tile(jnp.concatenate([-jnp.sin(ang), jnp.sin(ang)], -1), (1, H_A))
    half_b = ROT_B // 2
    inv_b = jnp.power(ROPE_THETA, -jnp.arange(half_b, dtype=F32) / half_b)
    angb = pos * inv_b[None, :]
    t = pos.shape[0]
    one = jnp.ones((t, HD_B - ROT_B), F32)
    zero = jnp.zeros((t, HD_B - ROT_B), F32)
    cos_b = jnp.tile(jnp.concatenate([jnp.cos(angb), jnp.cos(angb), one], -1), (1, 2))
    sin_b = jnp.tile(jnp.concatenate([-jnp.sin(angb), jnp.sin(angb), zero], -1), (1, 2))
    return cos_a, sin_a, cos_b, sin_b


def _ret_tables(c):
    lg = jnp.log1p(-jnp.power(2.0, -5.0 - jnp.arange(H_A, dtype=F32)))
    idx = jnp.arange(c, dtype=F32)
    rel = idx[:, None] - idx[None, :]
    dmask = jnp.where(rel[None] >= 0, jnp.exp(jnp.maximum(rel, 0.0)[None] * lg[:, None, None]), 0.0)
    dcat = jnp.transpose(dmask, (1, 0, 2)).reshape(c, H_A * c)
    head_of_row = jnp.repeat(jnp.arange(H_A), c)
    hmask = (head_of_row[:, None] == (jnp.arange(128) // DK_A)[None, :]).astype(F32)
    vmask = (head_of_row[:, None] == (jnp.arange(W_A) // DV_A)[None, :]).astype(F32)
    lg_lane = jnp.repeat(lg, DK_A)
    qdec = jnp.exp((idx[:, None] + 1.0) * lg_lane[None, :])
    kdec = jnp.exp((c - 1.0 - idx)[:, None] * lg_lane[None, :])
    bd_a = ((jnp.arange(128) // DK_A)[:, None] == (jnp.arange(W_A) // DV_A)[None, :]).astype(F32)
    cdec = jnp.exp(c * lg_lane)[:, None] * bd_a
    return dict(hmask=hmask, dcat=dcat, vmask=vmask, qdec=qdec, kdec=kdec, cdec=cdec, bd_a=bd_a)


RET_TABLE_NAMES = ("hmask", "dcat", "vmask", "qdec", "kdec", "cdec", "bd_a")


def _block_ones_64():
    g = jnp.arange(256) // 64
    return (g[:, None] == g[None, :])


def _pool_weight_bd(w_pool_l):
    out = jnp.zeros((W_C, W_C), F32)
    for g in range(len(POOL_WINDOWS)):
        out = out.at[g * GC:(g + 1) * GC, g * GC:(g + 1) * GC].set(w_pool_l[g])
    return out.astype(BF16)


def _prompt_layer_body(layer, tt, *refs):
    (x_ref, cosa_ref, sina_ref, cosb_ref, sinb_ref,
     hmask_ref, dcat_ref, vmask_ref, qdec_ref, kdec_ref, cdec_ref, bda_ref,
     hga_ref, hglvl_ref, hgvm_ref,
     ones_ref, bdd_ref, win_ref, wbr_ref, wout_ref, gpre_ref, gpost_ref, sink_ref,
     wpool_ref, pscale_ref, ghgrn_ref, lbnd_ref,
     xo_ref, ret_ref, ko_ref, vo_ref, poolo_ref, hgo_ref,
     hb_ref, y_ref, sbd_ref, stbd_ref, kprev_ref, vprev_ref, ext_ref) = refs
    t_i = pl.program_id(1)
    n_t = pl.num_programs(1)

    @pl.when(t_i == 0)
    def _():
        sbd_ref[...] = jnp.zeros_like(sbd_ref)
        stbd_ref[...] = jnp.zeros_like(stbd_ref)
        kprev_ref[...] = jnp.zeros_like(kprev_ref)
        vprev_ref[...] = jnp.zeros_like(vprev_ref)
        ext_ref[0:16, :] = jnp.zeros((16, W_C), F32)

    x = x_ref[0]
    hb_ref[...] = (_rms(x) * gpre_ref[...]).astype(BF16)
    ones_bd = ones_ref[...]
    rt = dict(hmask=hmask_ref[...], dcat=dcat_ref[...], vmask=vmask_ref[...], qdec=qdec_ref[...],
              kdec=kdec_ref[...], cdec=cdec_ref[...], bd_a=bda_ref[...])
    nblk = tt // RET_CHUNK

    pa = jnp.dot(hb_ref[...], win_ref[:, QA:ZA + W_A], preferred_element_type=F32)
    qa = _rope(pa[:, QA:QA + 128], cosa_ref[...], sina_ref[...], DK_A // 2, DK_A)
    ka = _rope(pa[:, KA:KA + 128], cosa_ref[...], sina_ref[...], DK_A // 2, DK_A) * (DK_A ** -0.5)
    va = pa[:, VA:VA + W_A]
    s_bd = sbd_ref[...]
    oa = []
    for c in range(nblk):
        r = slice(c * RET_CHUNK, (c + 1) * RET_CHUNK)
        o_c, s_bd = _retention_chunk(qa[r], ka[r], va[r], s_bd, rt)
        oa.append(o_c)
    sbd_ref[...] = s_bd
    oa = jnp.concatenate(oa, axis=0)
    oa = oa * lax.rsqrt(_group_mean_sq(oa, ones_bd) + EPS)
    y_ref[:, Y_OFF[0]:Y_OFF[0] + W_A] = (oa * _silu(pa[:, ZA:ZA + W_A])).astype(BF16)

    pb = jnp.dot(hb_ref[...], win_ref[:, QB:ZB + W_B], preferred_element_type=F32)
    cosb, sinb = cosb_ref[...], sinb_ref[...]
    qb = _rope(pb[:, 0:W_B], jnp.concatenate([cosb] * 4, axis=1), jnp.concatenate([sinb] * 4, axis=1),
               ROT_B // 2, HD_B)
    kb = _rope(pb[:, KB - QB:KB - QB + 128], cosb, sinb, ROT_B // 2, HD_B)
    vb = pb[:, VB - QB:VB - QB + 128]
    sink_col = sink_ref[...]
    ob = []
    k_prev, v_prev = kprev_ref[...], vprev_ref[...]
    for c in range(nblk):
        r = slice(c * WINDOW, (c + 1) * WINDOW)
        lim = jnp.where(t_i == 0, WINDOW, 0) if c == 0 else 0
        valid = _swa_valid(WINDOW, lim)
        ob.append(_swa_block(qb[r], kb[r], vb[r], k_prev, v_prev, sink_col, valid))
        k_prev, v_prev = kb[r], vb[r]
    kprev_ref[...] = k_prev
    vprev_ref[...] = v_prev
    ob = jnp.concatenate(ob, axis=0)
    y_ref[:, Y_OFF[1]:Y_OFF[1] + W_B] = (ob * _silu(pb[:, ZB - QB:ZB - QB + W_B])).astype(BF16)

    pc = jnp.dot(hb_ref[...], win_ref[:, UC:ZC + W_C], preferred_element_type=F32)
    ext_ref[16:16 + tt, :] = pc[:, 0:W_C]
    oc = _pool_tile(ext_ref[...], t_i * tt, wpool_ref[...], pscale_ref[...])
    y_ref[:, Y_OFF[2]:Y_OFF[2] + W_C] = (oc * _silu(pc[:, W_C:2 * W_C])).astype(BF16)

    ext_ref[0:16, :] = ext_ref[tt:tt + 16, :]

    pd = jnp.dot(hb_ref[...], win_ref[:, QD:ZD + W_D], preferred_element_type=F32)
    lb = _lower_bound_row(lbnd_ref[...], layer)
    ht = dict(hg_a=hga_ref[...], hg_lvl=hglvl_ref[...], hg_vmask=hgvm_ref[...])
    fg = lb + (1.0 - lb) * _sigmoid(pd[:, 256:512])
    qh = _silu(pd[:, 0:256])
    kh = 1.0 - fg
    lfh = jnp.log(fg)
    vh = pd[:, 512:768]
    st_bd = stbd_ref[...]
    od = []
    for c in range(tt // HG_BLOCK):
        r = slice(c * HG_BLOCK, (c + 1) * HG_BLOCK)
        o_c, st_bd = _hgrn_block_tree(qh[r], kh[r], vh[r], lfh[r], st_bd, ht, ones_bd, bdd_ref[...])
        od.append(o_c)
    od = jnp.concatenate(od, axis=0)
    stbd_ref[...] = st_bd
    od = od * lax.rsqrt(_group_mean_sq(od, ones_bd) + EPS) * ghgrn_ref[...]
    y_ref[:, Y_OFF[3]:Y_OFF[3] + W_D] = (od * _silu(pd[:, 768:1024])).astype(BF16)

    merged = None
    for i in range(4):
        g = _sigmoid(jnp.dot(hb_ref[...], win_ref[:, GL + i * D_MODEL:GL + (i + 1) * D_MODEL],
                             preferred_element_type=F32))
        term = g * jnp.dot(y_ref[:, Y_OFF[i]:Y_OFF[i] + BRANCH_WIDTHS[i]],
                           wbr_ref[Y_OFF[i]:Y_OFF[i] + BRANCH_WIDTHS[i], :], preferred_element_type=F32)
        merged = term if merged is None else merged + term
    out = jnp.dot(merged.astype(BF16), wout_ref[...], preferred_element_type=F32)
    xo_ref[0] = x + _rms(out) * gpost_ref[...]

    @pl.when(t_i == n_t - 1)
    def _():
        s = sbd_ref[...]
        ret_ref[0] = s[:, 0:64] + s[:, 64:128] + s[:, 128:192] + s[:, 192:256]
        st = stbd_ref[...]
        hgo_ref[0] = (st[0:64] + st[64:128] + st[128:192] + st[192:256]).T
        ko_ref[0] = kprev_ref[...]
        vo_ref[0] = vprev_ref[...]
        poolo_ref[0] = ext_ref[pl.ds(1, POOL_HIST), :]


def _const_spec(shape):
    nd = len(shape)
    return pl.BlockSpec(shape, lambda b, t: (0,) * nd, pipeline_mode=pl.Buffered(1))


def _prompt_layer(layer, x, tabs, consts, w_in, w_br, w_out, g_pre, g_post, sink_col, wpool_bd, pscale,
                  g_hgrn, lower_bounds):
    bsz, seq, _ = x.shape
    tt = PROMPT_TILE
    nt = seq // tt
    cos_a, sin_a, cos_b, sin_b = tabs
    rt, ones_bd, bd_d, hg = consts
    tab_spec = pl.BlockSpec((tt, LANES), lambda b, t: (t, 0))
    in_arrays = [x, cos_a, sin_a, cos_b, sin_b] + [rt[n] for n in RET_TABLE_NAMES] + [
        hg[n] for n in HG_TABLE_NAMES] + [
        ones_bd, bd_d, w_in, w_br, w_out, g_pre, g_post, sink_col, wpool_bd, pscale, g_hgrn, lower_bounds]
    in_specs = [pl.BlockSpec((1, tt, D_MODEL), lambda b, t: (b, t, 0)), tab_spec, tab_spec, tab_spec, tab_spec]
    in_specs += [_const_spec(a.shape) for a in in_arrays[5:]]
    out_shape = (
        jax.ShapeDtypeStruct((bsz, seq, D_MODEL), F32),
        jax.ShapeDtypeStruct((bsz, H_A * DK_A, DV_A), F32),
        jax.ShapeDtypeStruct((bsz, WINDOW, KV_B * HD_B), F32),
        jax.ShapeDtypeStruct((bsz, WINDOW, KV_B * HD_B), F32),
        jax.ShapeDtypeStruct((bsz, POOL_HIST, W_C), F32),
        jax.ShapeDtypeStruct((bsz, H_D * DK_D, DV_D), F32),
    )
    out_specs = (
        pl.BlockSpec((1, tt, D_MODEL), lambda b, t: (b, t, 0)),
        pl.BlockSpec((1, H_A * DK_A, DV_A), lambda b, t: (b, 0, 0)),
        pl.BlockSpec((1, WINDOW, KV_B * HD_B), lambda b, t: (b, 0, 0)),
        pl.BlockSpec((1, WINDOW, KV_B * HD_B), lambda b, t: (b, 0, 0)),
        pl.BlockSpec((1, POOL_HIST, W_C), lambda b, t: (b, 0, 0)),
        pl.BlockSpec((1, H_D * DK_D, DV_D), lambda b, t: (b, 0, 0)),
    )
    scratch = [
        pltpu.VMEM((tt, D_MODEL), BF16),
        pltpu.VMEM((tt, W_MIX), BF16),
        pltpu.VMEM((H_A * DK_A, W_A), F32),
        pltpu.VMEM((W_D, H_D * DK_D), F32),
        pltpu.VMEM((WINDOW, KV_B * HD_B), F32),
        pltpu.VMEM((WINDOW, KV_B * HD_B), F32),
        pltpu.VMEM((16 + tt, W_C), F32),
    ]
    return pl.pallas_call(
        functools.partial(_prompt_layer_body, layer, tt),
        grid=(bsz, nt),
        in_specs=in_specs,
        out_specs=out_specs,
        out_shape=out_shape,
        scratch_shapes=scratch,
        compiler_params=pltpu.CompilerParams(
            dimension_semantics=("arbitrary", "arbitrary"), vmem_limit_bytes=VMEM_LIMIT),
        name=f"prompt_layer{layer}",
    )(*in_arrays)


def _sample_inproj_body(x_ref, gpre_ref, w_ref, o_ref):
    hb = (_rms(x_ref[...]) * gpre_ref[...]).astype(BF16)
    o_ref[...] = jnp.dot(hb, w_ref[...], preferred_element_type=F32)


def _sample_inproj(layer, x2, g_pre, w_in):
    n = x2.shape[0]
    cb = 512
    return pl.pallas_call(
        _sample_inproj_body,
        grid=(D_IN // cb,),
        in_specs=[pl.BlockSpec((n, D_MODEL), lambda j: (0, 0)),
                  pl.BlockSpec((1, D_MODEL), lambda j: (0, 0)),
                  pl.BlockSpec((D_MODEL, cb), lambda j: (0, j))],
        out_specs=pl.BlockSpec((n, cb), lambda j: (0, j)),
        out_shape=jax.ShapeDtypeStruct((n, D_IN), F32),
        compiler_params=pltpu.CompilerParams(dimension_semantics=("arbitrary",), vmem_limit_bytes=VMEM_LIMIT),
        name=f"sample_inproj{layer}",
    )(x2, g_pre, w_in)


def _sample_mixer_body(layer, tq, *refs):
    (p_ref, cosa_ref, sina_ref, cosb_ref, sinb_ref,
     hmask_ref, dcat_ref, vmask_ref, qdec_ref, kdec_ref, cdec_ref, bda_ref,
     ones_ref, bdd_ref, sink_ref, wpool_ref, pscale_ref, ghgrn_ref, lbnd_ref,
     sret_ref, ck_ref, cv_ref, hist_ref, shg_ref,
     y_ref, reto_ref, ko_ref, vo_ref, poolo_ref, hgo_ref) = refs
    p = p_ref[0]
    ones_bd = ones_ref[...]
    rt = dict(hmask=hmask_ref[...], dcat=dcat_ref[...], vmask=vmask_ref[...], qdec=qdec_ref[...],
              kdec=kdec_ref[...], cdec=cdec_ref[...], bd_a=bda_ref[...])

    qa = _rope(p[:, QA:QA + 128], cosa_ref[...], sina_ref[...], DK_A // 2, DK_A)
    ka = _rope(p[:, KA:KA + 128], cosa_ref[...], sina_ref[...], DK_A // 2, DK_A) * (DK_A ** -0.5)
    va = p[:, VA:VA + W_A]
    s_bd = jnp.concatenate([sret_ref[0]] * H_A, axis=1) * rt["bd_a"]
    oa, s_bd = _retention_chunk(qa, ka, va, s_bd, rt)
    reto_ref[0] = s_bd[:, 0:64] + s_bd[:, 64:128] + s_bd[:, 128:192] + s_bd[:, 192:256]
    oa = oa * lax.rsqrt(_group_mean_sq(oa, ones_bd) + EPS)
    ya = oa * _silu(p[:, ZA:ZA + W_A])

    cosb, sinb = cosb_ref[...], sinb_ref[...]
    qb = _rope(p[:, QB:QB + W_B], jnp.concatenate([cosb] * 4, axis=1), jnp.concatenate([sinb] * 4, axis=1),
               ROT_B // 2, HD_B)
    kb = _rope(p[:, KB:KB + 128], cosb, sinb, ROT_B // 2, HD_B)
    vb = p[:, VB:VB + 128]
    k_prev, v_prev = ck_ref[0], cv_ref[0]
    ob = _swa_block(qb, kb, vb, k_prev, v_prev, sink_ref[...], _swa_valid(tq, 0))
    ko_ref[0] = jnp.concatenate([k_prev[tq:], kb], axis=0)
    vo_ref[0] = jnp.concatenate([v_prev[tq:], vb], axis=0)
    yb = ob * _silu(p[:, ZB:ZB + W_B])

    u = p[:, UC:UC + W_C]
    ext = jnp.concatenate([jnp.zeros((1, W_C), F32), hist_ref[0], u], axis=0)
    oc = _pool_tile(ext, PAST_LEN, wpool_ref[...], pscale_ref[...])
    poolo_ref[0] = ext[tq + 1:tq + 16]
    yc = oc * _silu(p[:, ZC:ZC + W_C])

    lb = _lower_bound_row(lbnd_ref[...], layer)
    bd_d = bdd_ref[...]
    st = shg_ref[0].T
    st_bd = jnp.concatenate([st] * H_D, axis=0) * bd_d
    od, st_bd = _hgrn_tile(p[:, QD:QD + 256], p[:, FD:FD + 256], p[:, ID:ID + 256], lb, st_bd,
                           math.gcd(tq, HGRN_CHUNK), ones_bd, bd_d)
    hgo_ref[0] = (st_bd[0:64] + st_bd[64:128] + st_bd[128:192] + st_bd[192:256]).T
    od = od * lax.rsqrt(_group_mean_sq(od, ones_bd) + EPS) * ghgrn_ref[...]
    yd = od * _silu(p[:, ZD:ZD + W_D])

    y_ref[0] = jnp.concatenate([ya, yb, yc, yd], axis=1)


def _sample_mixer(layer, proj3, tabs, consts, sink_col, wpool_bd, pscale, g_hgrn, lower_bounds,
                  s_ret, c_k, c_v, hist, s_hg):
    nb, tq, _ = proj3.shape
    cos_a, sin_a, cos_b, sin_b = tabs
    rt, ones_bd, bd_d = consts
    const_arrays = [cos_a, sin_a, cos_b, sin_b] + [rt[n] for n in RET_TABLE_NAMES] + [
        ones_bd, bd_d, sink_col, wpool_bd, pscale, g_hgrn, lower_bounds]

    def cspec(a):
        nd = a.ndim
        return pl.BlockSpec(a.shape, lambda b: (0,) * nd)

    def bspec(a):
        nd = a.ndim
        return pl.BlockSpec((1,) + a.shape[1:], lambda b: (b,) + (0,) * (nd - 1))

    state_arrays = [s_ret, c_k, c_v, hist, s_hg]
    out_shape = (jax.ShapeDtypeStruct((nb, tq, W_MIX), F32),) + tuple(
        jax.ShapeDtypeStruct(a.shape, F32) for a in state_arrays)
    return pl.pallas_call(
        functools.partial(_sample_mixer_body, layer, tq),
        grid=(nb,),
        in_specs=[bspec(proj3)] + [cspec(a) for a in const_arrays] + [bspec(a) for a in state_arrays],
        out_specs=tuple(bspec(s) for s in out_shape),
        out_shape=out_shape,
        compiler_params=pltpu.CompilerParams(dimension_semantics=("arbitrary",), vmem_limit_bytes=VMEM_LIMIT),
        name=f"sample_mixer{layer}",
    )(proj3, *const_arrays, *state_arrays)


def _sample_merge_body(x_ref, y_ref, gl_ref, wbr_ref, wout_ref, gpost_ref, o_ref):
    merged = None
    for i in range(4):
        g = _sigmoid(gl_ref[:, i * D_MODEL:(i + 1) * D_MODEL])
        term = g * jnp.dot(y_ref[:, Y_OFF[i]:Y_OFF[i] + BRANCH_WIDTHS[i]].astype(BF16),
                           wbr_ref[Y_OFF[i]:Y_OFF[i] + BRANCH_WIDTHS[i], :], preferred_element_type=F32)
        merged = term if merged is None else merged + term
    out = jnp.dot(merged.astype(BF16), wout_ref[...], preferred_element_type=F32)
    o_ref[...] = x_ref[...] + _rms(out) * gpost_ref[...]


def _sample_merge(layer, x2, y2, gl, w_br, w_out, g_post):
    return pl.pallas_call(
        _sample_merge_body,
        out_shape=jax.ShapeDtypeStruct(x2.shape, F32),
        compiler_params=pltpu.CompilerParams(vmem_limit_bytes=VMEM_LIMIT),
        name=f"sample_merge{layer}",
    )(x2, y2, gl, w_br, w_out, g_post)


def kernel(x_prompt, x_sample, state_ret, cache_swa_k, cache_swa_v, state_pool, state_hgrn, w_in, w_branch,
           w_out, g_pre, g_post, attn_sink, w_pool, pool_scale, g_hgrn, lower_bounds):
    bsz, seq, _ = x_prompt.shape
    nb, tq, _ = x_sample.shape
    w_in_b = w_in.astype(BF16)
    w_br_b = w_branch.astype(BF16)
    w_out_b = w_out.astype(BF16)
    ones_bd = _block_ones_64().astype(BF16)
    bd_d = _block_ones_64().astype(F32)
    consts_p = (_ret_tables(RET_CHUNK), ones_bd, bd_d, _hgrn_tables())
    consts_s = (_ret_tables(math.gcd(tq, RET_CHUNK)), ones_bd, bd_d)
    tabs_p = _rope_tables(jnp.arange(seq))
    tabs_s = _rope_tables(PAST_LEN + jnp.arange(tq))

    def sink_cols(sink_l, rows):
        return jnp.repeat(sink_l.reshape(KV_B, G_B), rows, axis=1)[..., None].astype(F32)

    xp = x_prompt
    xs = x_sample.reshape(nb * tq, D_MODEL)
    acc_p = [[] for _ in range(5)]
    acc_s = [[] for _ in range(5)]
    for l in range(DEPTH):
        wpool_bd = _pool_weight_bd(w_pool[l])
        pscale = pool_scale[l][None, :]
        ghg = g_hgrn[l][None, :]
        gpre = g_pre[l][None, :]
        gpost = g_post[l][None, :]

        xp, r_p, k_p, v_p, pool_p, h_p = _prompt_layer(
            l, xp, tabs_p, consts_p, w_in_b[l], w_br_b[l], w_out_b[l], gpre, gpost,
            sink_cols(attn_sink[l], WINDOW), wpool_bd, pscale, ghg, lower_bounds)
        for a, s in zip(acc_p, (r_p.reshape(bsz, H_A, DK_A, DV_A), k_p.reshape(bsz, WINDOW, KV_B, HD_B),
                                v_p.reshape(bsz, WINDOW, KV_B, HD_B), pool_p,
                                h_p.reshape(bsz, H_D, DK_D, DV_D))):
            a.append(s)

        proj = _sample_inproj(l, xs, gpre, w_in_b[l])
        y3, r_s, k_s, v_s, pool_s, h_s = _sample_mixer(
            l, proj[:, :GL].reshape(nb, tq, GL), tabs_s, consts_s, sink_cols(attn_sink[l], tq), wpool_bd,
            pscale, ghg, lower_bounds,
            state_ret[l].reshape(nb, H_A * DK_A, DV_A), cache_swa_k[l].reshape(nb, WINDOW, KV_B * HD_B),
            cache_swa_v[l].reshape(nb, WINDOW, KV_B * HD_B), state_pool[l],
            state_hgrn[l].reshape(nb, H_D * DK_D, DV_D))
        xs = _sample_merge(l, xs, y3.reshape(nb * tq, W_MIX), proj[:, GL:], w_br_b[l], w_out_b[l], gpost)
        for a, s in zip(acc_s, (r_s.reshape(nb, H_A, DK_A, DV_A), k_s.reshape(nb, WINDOW, KV_B, HD_B),
                                v_s.reshape(nb, WINDOW, KV_B, HD_B), pool_s,
                                h_s.reshape(nb, H_D, DK_D, DV_D))):
            a.append(s)

    outs_p = [jnp.stack(a) for a in acc_p]
    outs_s = [jnp.stack(a) for a in acc_s]
    return (xp, xs.reshape(nb, tq, D_MODEL), *outs_p, *outs_s)
```

```python
import functools
import math

import numpy as np
import jax
import jax.numpy as jnp
from jax import lax
from jax.experimental import pallas as pl
from jax.experimental.pallas import tpu as pltpu

F32 = jnp.float32
BF16 = jnp.bfloat16

D_MODEL = 1024
DEPTH = 4
PAST_LEN = 8192
H_A, DK_A, DV_A = 4, 32, 64
RET_CHUNK = 128
RET_THETA = 10000.0
H_B, KV_B, HD_B = 8, 2, 64
G_B = H_B // KV_B
WINDOW = 128
ROPE_THETA = 500000.0
ROT_B = HD_B // 4
W_C = 256
POOL_WINDOWS = (2, 4, 8, 16)
GC = 64
POOL_HIST = 15
H_D, DK_D, DV_D = 4, 64, 64
HGRN_CHUNK = 16
W_A, W_B, W_D = H_A * DV_A, H_B * HD_B, H_D * DV_D
BRANCH_WIDTHS = (W_A, W_B, W_C, W_D)
W_MIX = sum(BRANCH_WIDTHS)
EPS = 1e-6

QA, KA, VA, ZA = 0, 128, 256, 512
QB, KB, VB, ZB = 768, 1280, 1408, 1536
UC, ZC = 2048, 2304
QD, FD, ID, ZD = 2560, 2816, 3072, 3328
GL = 3584
D_IN = GL + 4 * D_MODEL
Y_OFF = (0, W_A, W_A + W_B, W_A + W_B + W_C)

LANES = 128
VMEM_LIMIT = 56 * 1024 * 1024
PROMPT_TILE = 256


def _dot(a, b):
    return jnp.dot(a.astype(BF16), b.astype(BF16), preferred_element_type=F32)


def _dot_nt(a, b):
    return lax.dot_general(a.astype(BF16), b.astype(BF16), (((1,), (1,)), ((), ())),
                           preferred_element_type=F32)


def _dot_tn(a, b):
    return lax.dot_general(a.astype(BF16), b.astype(BF16), (((0,), (0,)), ((), ())),
                           preferred_element_type=F32)


def _sigmoid(x):
    return 1.0 / (1.0 + jnp.exp(-x))


def _silu(x):
    return x * _sigmoid(x)


def _rms(x):
    return x * lax.rsqrt(jnp.mean(x * x, axis=-1, keepdims=True) + EPS)


def _shift_rows(x, j):
    m = x.shape[0]
    if m % 8 == 0:
        return pltpu.roll(x, j, 0)
    return jnp.concatenate([jnp.zeros((j, x.shape[1]), x.dtype), x[:m - j]], axis=0)


def _rope(x, cos, sin, half, group):
    n = x.shape[1]
    lane = lax.broadcasted_iota(jnp.int32, x.shape, 1)
    first = (lane % group) < half
    swapped = jnp.where(first, pltpu.roll(x, n - half, 1), pltpu.roll(x, half, 1))
    return x * cos + swapped * sin


def _group_mean_sq(o, ones_bd):
    sq = o * o
    hi = sq.astype(BF16)
    lo = (sq - hi.astype(F32)).astype(BF16)
    tot = (jnp.dot(hi, ones_bd, preferred_element_type=F32)
           + jnp.dot(lo, ones_bd, preferred_element_type=F32))
    return tot * (1.0 / 64.0)


def _retention_chunk(q, k, v, s_bd, t):
    kexp = jnp.concatenate([k] * H_A, axis=0) * t["hmask"]
    sc = _dot_nt(q, kexp) * t["dcat"]
    vexp = jnp.concatenate([v] * H_A, axis=0) * t["vmask"]
    lhs = jnp.concatenate([q * t["qdec"], sc], axis=1)
    rhs = jnp.concatenate([s_bd, vexp], axis=0)
    o = _dot(lhs, rhs)
    s_new = t["cdec"] * s_bd + t["bd_a"] * _dot_tn(k * t["kdec"], v)
    return o, s_new


def _swa_block(q, k_new, v_new, k_prev, v_prev, sink_col, valid):
    bq = q.shape[0]
    bias = jnp.where(valid, 0.0, -jnp.inf).astype(F32)
    bias = jnp.concatenate([bias] * G_B, axis=0)
    outs = []
    for kv in range(KV_B):
        lo = kv * HD_B
        qs = jnp.concatenate(
            [q[:, (kv * G_B + g) * HD_B:(kv * G_B + g + 1) * HD_B] for g in range(G_B)], axis=0)
        qs = qs * (HD_B ** -0.5)
        keys = jnp.concatenate([k_prev[kv], k_new[:, lo:lo + HD_B]], axis=0)
        vals = jnp.concatenate([v_prev[kv], v_new[:, lo:lo + HD_B]], axis=0)
        s = _dot_nt(qs, keys) + bias
        sk = sink_col[kv]
        m = jnp.maximum(jnp.max(s, axis=1, keepdims=True), sk)
        e = jnp.exp(s - m)
        vext = jnp.concatenate([vals, jnp.ones_like(vals)], axis=1)
        oe = _dot(e, vext)
        den = oe[:, HD_B:] + jnp.exp(sk - m)
        o = oe[:, :HD_B] / den
        outs.append(jnp.concatenate([o[g * bq:(g + 1) * bq] for g in range(G_B)], axis=1))
    return jnp.concatenate(outs, axis=1)


def _pool_tile(ext, pos0, wpool_bd, pool_scale):
    return _dot(_pool_sums(ext, pos0), wpool_bd) * pool_scale


def _pool_sums(ext, pos0):
    m = ext.shape[0] - 16
    s2 = ext[1:] + ext[:-1]
    s4 = s2[2:] + s2[:-2]
    s8 = s4[4:] + s4[:-4]
    s16 = s8[8:] + s8[:-8]
    u = ext[16:]
    lane = lax.broadcasted_iota(jnp.int32, (m, W_C), 1)
    grp = lane // GC
    win = jnp.where(grp == 0, s2[15:15 + m],
                    jnp.where(grp == 1, s4[13:13 + m],
                              jnp.where(grp == 2, s8[9:9 + m], s16[1:1 + m])))
    wlen = jnp.where(grp == 0, 2, jnp.where(grp == 1, 4, jnp.where(grp == 2, 8, 16)))
    pos = pos0 + lax.broadcasted_iota(jnp.int32, (m, W_C), 0)
    cnt = jnp.minimum(pos + 1, wlen).astype(F32)
    return win / cnt - u


HG_BLOCK = 128
HG_LEVELS = 7
HG_MXU_LEVELS = 3


def _hgrn_block_tree(q, k, v, lf, st_bd, t, ones_bd, bd_d):
    n = HG_BLOCK
    hi = lf.astype(BF16)
    r1 = lf - hi.astype(F32)
    mid = r1.astype(BF16)
    lo = (r1 - mid.astype(F32)).astype(BF16)
    amat = t["hg_a"]
    amat2 = jnp.concatenate([amat, amat], axis=1)
    ex = (jnp.dot(amat2, jnp.concatenate([hi, mid], axis=0), preferred_element_type=F32)
          + jnp.dot(amat, lo, preferred_element_type=F32))
    b = ex[0:n]
    blast = jnp.broadcast_to(b[n - 1:n], (n, W_D))
    qdec = q * jnp.exp(b)
    kdec = k * jnp.exp(blast - b)
    row = lax.broadcasted_iota(jnp.int32, (n, W_D), 0)
    lane = lax.broadcasted_iota(jnp.int32, (n, LANES), 1)
    lvl = t["hg_lvl"]
    xs = [jnp.zeros((n, n), F32) for _ in range(H_D)]
    for i in range(HG_LEVELS):
        s = 1 << i
        if i < HG_MXU_LEVELS:
            ex_s = ex[(1 + i) * n:(2 + i) * n]
        else:
            bmid = jnp.concatenate(
                [jnp.broadcast_to(b[j * 2 * s + s - 1:j * 2 * s + s], (2 * s, W_D)) for j in range(n // (2 * s))],
                axis=0)
            d = b - bmid
            ex_s = jnp.minimum(d, -d)
        second = (row // s) % 2 == 1
        p = jnp.where(second, q, k) * jnp.exp(ex_s)
        qt = jnp.where(second, p, 0.0).astype(BF16)
        kt = jnp.where(second, 0.0, p).astype(BF16)
        for pr in range(H_D // 2):
            c0 = pr * LANES
            ktp = kt[:, c0:c0 + LANES]
            zero = jnp.zeros((), BF16)
            kst = jnp.concatenate([jnp.where((lane // DK_D) == 0, ktp, zero),
                                   jnp.where((lane // DK_D) == 1, ktp, zero)], axis=0)
            x = lax.dot_general(qt[:, c0:c0 + LANES], kst, (((1,), (1,)), ((), ())),
                                preferred_element_type=F32)
            for hh in range(2):
                h = 2 * pr + hh
                xs[h] = jnp.where(lvl == i, x[:, hh * n:(hh + 1) * n], xs[h])
    vb = v.astype(BF16)
    vexp = jnp.concatenate([vb] * H_D, axis=0) * t["hg_vmask"]
    sc = jnp.concatenate(xs, axis=1).astype(BF16)
    o = jnp.dot(sc, vexp, preferred_element_type=F32)
    o = o + _dot(q * k, ones_bd) * v
    o = o + _dot_nt(qdec, st_bd)
    st_new = jnp.exp(b[n - 1:n]) * st_bd + bd_d * _dot_tn(v, kdec)
    return o, st_new


def _hgrn_tables():
    n = HG_BLOCK
    t = np.arange(n)[:, None]
    u = np.arange(n)[None, :]
    mats = [(u <= t)]
    for i in range(HG_MXU_LEVELS):
        s = 1 << i
        midp = (t // (2 * s)) * 2 * s + s - 1
        second = (t % (2 * s)) >= s
        mats.append(np.where(second, (u > midp) & (u <= t), (u > t) & (u <= midp)))
    amat = np.concatenate(mats, axis=0).astype(np.float32)
    x = np.bitwise_xor(t, u)
    lvl = np.where(u < t, np.floor(np.log2(np.maximum(x, 1))).astype(np.int32), -1).astype(np.int32)
    head_of_row = np.repeat(np.arange(H_D), n)
    vmask = (head_of_row[:, None] == (np.arange(W_D) // DV_D)[None, :]).astype(np.float32)
    return dict(hg_a=jnp.asarray(amat, BF16), hg_lvl=jnp.asarray(lvl), hg_vmask=jnp.asarray(vmask, BF16))


HG_TABLE_NAMES = ("hg_a", "hg_lvl", "hg_vmask")


def _lower_bound_row(lower_bounds, layer):
    x = lower_bounds
    mx = jnp.max(x, axis=0, keepdims=True)
    e = jnp.exp(x - mx)
    tot = jnp.sum(e, axis=0, keepdims=True)
    if layer == 0:
        return jnp.zeros_like(tot)
    acc = e[1:2]
    for i in range(2, layer + 1):
        acc = acc + e[i:i + 1]
    return acc / tot


def _swa_valid(bq, lim):
    row = lax.broadcasted_iota(jnp.int32, (bq, WINDOW + bq), 0)
    col = lax.broadcasted_iota(jnp.int32, (bq, WINDOW + bq), 1)
    ok = jnp.where(col >= row, jnp.where(col <= row + WINDOW, 1, 0), 0)
    ok = jnp.where(col >= lim, ok, 0)
    return ok > 0


def _rope_tables(pos):
    pos = pos.astype(F32)[:, None]
    half_a = DK_A // 2
    inv_a = jnp.power(RET_THETA, -jnp.arange(half_a, dtype=F32) / half_a)
    ang = pos * inv_a[None, :]
    cos_a = jnp.tile(jnp.concatenate([jnp.cos(ang), jnp.cos(ang)], -1), (1, H_A))
    sin_a = jnp.tile(jnp.concatenate([-jnp.sin(ang), jnp.sin(ang)], -1), (1, H_A))
    half_b = ROT_B // 2
    inv_b = jnp.power(ROPE_THETA, -jnp.arange(half_b, dtype=F32) / half_b)
    angb = pos * inv_b[None, :]
    t = pos.shape[0]
    one = jnp.ones((t, HD_B - ROT_B), F32)
    zero = jnp.zeros((t, HD_B - ROT_B), F32)
    cos_b = jnp.tile(jnp.concatenate([jnp.cos(angb), jnp.cos(angb), one], -1), (1, 2))
    sin_b = jnp.tile(jnp.concatenate([-jnp.sin(angb), jnp.sin(angb), zero], -1), (1, 2))
    return cos_a, sin_a, cos_b, sin_b


def _ret_tables(c):
    lg = jnp.log1p(-jnp.power(2.0, -5.0 - jnp.arange(H_A, dtype=F32)))
    idx = jnp.arange(c, dtype=F32)
    rel = idx[:, None] - idx[None, :]
    dmask = jnp.where(rel[None] >= 0, jnp.exp(jnp.maximum(rel, 0.0)[None] * lg[:, None, None]), 0.0)
    dcat = jnp.transpose(dmask, (1, 0, 2)).reshape(c, H_A * c)
    head_of_row = jnp.repeat(jnp.arange(H_A), c)
    hmask = (head_of_row[:, None] == (jnp.arange(128) // DK_A)[None, :]).astype(F32)
    vmask = (head_of_row[:, None] == (jnp.arange(W_A) // DV_A)[None, :]).astype(F32)
    lg_lane = jnp.repeat(lg, DK_A)
    qdec = jnp.exp((idx[:, None] + 1.0) * lg_lane[None, :])
    kdec = jnp.exp((c - 1.0 - idx)[:, None] * lg_lane[None, :])
    bd_a = ((jnp.arange(128) // DK_A)[:, None] == (jnp.arange(W_A) // DV_A)[None, :]).astype(F32)
    cdec = jnp.exp(c * lg_lane)[:, None] * bd_a
    return dict(hmask=hmask, dcat=dcat, vmask=vmask, qdec=qdec, kdec=kdec, cdec=cdec, bd_a=bd_a)


RET_TABLE_NAMES = ("hmask", "dcat", "vmask", "qdec", "kdec", "cdec", "bd_a")


def _block_ones_64():
    g = jnp.arange(256) // 64
    return (g[:, None] == g[None, :])


def _pool_weight_bd(w_pool_l):
    out = jnp.zeros((W_C, W_C), F32)
    for g in range(len(POOL_WINDOWS)):
        out = out.at[g * GC:(g + 1) * GC, g * GC:(g + 1) * GC].set(w_pool_l[g])
    return out.astype(BF16)


def _prompt_layer_body(layer, tt, *refs):
    (x_ref, cosa_ref, sina_ref, cosb_ref, sinb_ref,
     hmask_ref, dcat_ref, vmask_ref, qdec_ref, kdec_ref, cdec_ref, bda_ref,
     hga_ref, hglvl_ref, hgvm_ref,
     ones_ref, bdd_ref, win_ref, wbr_ref, wout_ref, gpre_ref, gpost_ref, sink_ref,
     wpool_ref, pscale_ref, ghgrn_ref, lbnd_ref,
     xo_ref, ret_ref, ko_ref, vo_ref, poolo_ref, hgo_ref,
     hb_ref, y_ref, sbd_ref, stbd_ref, kprev_ref, vprev_ref, ext_ref) = refs
    t_i = pl.program_id(1)
    n_t = pl.num_programs(1)

    @pl.when(t_i == 0)
    def _():
        sbd_ref[...] = jnp.zeros_like(sbd_ref)
        stbd_ref[...] = jnp.zeros_like(stbd_ref)
        kprev_ref[...] = jnp.zeros_like(kprev_ref)
        vprev_ref[...] = jnp.zeros_like(vprev_ref)
        ext_ref[0:16, :] = jnp.zeros((16, W_C), F32)

    x = x_ref[0]
    hb_ref[...] = (_rms(x) * gpre_ref[...]).astype(BF16)
    ones_bd = ones_ref[...]
    rt = dict(hmask=hmask_ref[...], dcat=dcat_ref[...], vmask=vmask_ref[...], qdec=qdec_ref[...],
              kdec=kdec_ref[...], cdec=cdec_ref[...], bd_a=bda_ref[...])
    nblk = tt // RET_CHUNK

    pa = jnp.dot(hb_ref[...], win_ref[:, QA:ZA + W_A], preferred_element_type=F32)
    qa = _rope(pa[:, QA:QA + 128], cosa_ref[...], sina_ref[...], DK_A // 2, DK_A)
    ka = _rope(pa[:, KA:KA + 128], cosa_ref[...], sina_ref[...], DK_A // 2, DK_A) * (DK_A ** -0.5)
    va = pa[:, VA:VA + W_A]
    s_bd = sbd_ref[...]
    oa = []
    for c in range(nblk):
        r = slice(c * RET_CHUNK, (c + 1) * RET_CHUNK)
        o_c, s_bd = _retention_chunk(qa[r], ka[r], va[r], s_bd, rt)
        oa.append(o_c)
    sbd_ref[...] = s_bd
    oa = jnp.concatenate(oa, axis=0)
    oa = oa * lax.rsqrt(_group_mean_sq(oa, ones_bd) + EPS)
    y_ref[:, Y_OFF[0]:Y_OFF[0] + W_A] = (oa * _silu(pa[:, ZA:ZA + W_A])).astype(BF16)

    pb = jnp.dot(hb_ref[...], win_ref[:, QB:ZB + W_B], preferred_element_type=F32)
    cosb, sinb = cosb_ref[...], sinb_ref[...]
    qb = _rope(pb[:, 0:W_B], jnp.concatenate([cosb] * 4, axis=1), jnp.concatenate([sinb] * 4, axis=1),
               ROT_B // 2, HD_B)
    kb = _rope(pb[:, KB - QB:KB - QB + 128], cosb, sinb, ROT_B // 2, HD_B)
    vb = pb[:, VB - QB:VB - QB + 128]
    sink_col = sink_ref[...]
    ob = []
    k_prev, v_prev = kprev_ref[...], vprev_ref[...]
    for c in range(nblk):
        r = slice(c * WINDOW, (c + 1) * WINDOW)
        lim = jnp.where(t_i == 0, WINDOW, 0) if c == 0 else 0
        valid = _swa_valid(WINDOW, lim)
        ob.append(_swa_block(qb[r], kb[r], vb[r], [k_prev[:, 0:HD_B], k_prev[:, HD_B:]],
                             [v_prev[:, 0:HD_B], v_prev[:, HD_B:]], sink_col, valid))
        k_prev, v_prev = kb[r], vb[r]
    kprev_ref[...] = k_prev
    vprev_ref[...] = v_prev
    ob = jnp.concatenate(ob, axis=0)
    y_ref[:, Y_OFF[1]:Y_OFF[1] + W_B] = (ob * _silu(pb[:, ZB - QB:ZB - QB + W_B])).astype(BF16)

    pc = jnp.dot(hb_ref[...], win_ref[:, UC:ZC + W_C], preferred_element_type=F32)
    ext_ref[16:16 + tt, :] = pc[:, 0:W_C]
    oc = _pool_tile(ext_ref[...], t_i * tt, wpool_ref[...], pscale_ref[...])
    y_ref[:, Y_OFF[2]:Y_OFF[2] + W_C] = (oc * _silu(pc[:, W_C:2 * W_C])).astype(BF16)

    ext_ref[0:16, :] = ext_ref[tt:tt + 16, :]

    pd = jnp.dot(hb_ref[...], win_ref[:, QD:ZD + W_D], preferred_element_type=F32)
    lb = _lower_bound_row(lbnd_ref[...], layer)
    ht = dict(hg_a=hga_ref[...], hg_lvl=hglvl_ref[...], hg_vmask=hgvm_ref[...])
    fg = lb + (1.0 - lb) * _sigmoid(pd[:, 256:512])
    qh = _silu(pd[:, 0:256])
    kh = 1.0 - fg
    lfh = jnp.log(fg)
    vh = pd[:, 512:768]
    st_bd = stbd_ref[...]
    od = []
    for c in range(tt // HG_BLOCK):
        r = slice(c * HG_BLOCK, (c + 1) * HG_BLOCK)
        o_c, st_bd = _hgrn_block_tree(qh[r], kh[r], vh[r], lfh[r], st_bd, ht, ones_bd, bdd_ref[...])
        od.append(o_c)
    od = jnp.concatenate(od, axis=0)
    stbd_ref[...] = st_bd
    od = od * lax.rsqrt(_group_mean_sq(od, ones_bd) + EPS) * ghgrn_ref[...]
    y_ref[:, Y_OFF[3]:Y_OFF[3] + W_D] = (od * _silu(pd[:, 768:1024])).astype(BF16)

    merged = None
    for i in range(4):
        g = _sigmoid(jnp.dot(hb_ref[...], win_ref[:, GL + i * D_MODEL:GL + (i + 1) * D_MODEL],
                             preferred_element_type=F32))
        term = g * jnp.dot(y_ref[:, Y_OFF[i]:Y_OFF[i] + BRANCH_WIDTHS[i]],
                           wbr_ref[Y_OFF[i]:Y_OFF[i] + BRANCH_WIDTHS[i], :], preferred_element_type=F32)
        merged = term if merged is None else merged + term
    out = jnp.dot(merged.astype(BF16), wout_ref[...], preferred_element_type=F32)
    xo_ref[0] = x + _rms(out) * gpost_ref[...]

    @pl.when(t_i == n_t - 1)
    def _():
        s = sbd_ref[...]
        ret_ref[0] = s[:, 0:64] + s[:, 64:128] + s[:, 128:192] + s[:, 192:256]
        st = stbd_ref[...]
        hgo_ref[0] = (st[0:64] + st[64:128] + st[128:192] + st[192:256]).T
        ko_ref[0] = kprev_ref[...]
        vo_ref[0] = vprev_ref[...]
        poolo_ref[0] = ext_ref[pl.ds(1, POOL_HIST), :]


def _const_spec(shape):
    nd = len(shape)
    return pl.BlockSpec(shape, lambda b, t: (0,) * nd, pipeline_mode=pl.Buffered(1))


def _prompt_layer(layer, x, tabs, consts, w_in, w_br, w_out, g_pre, g_post, sink_col, wpool_bd, pscale,
                  g_hgrn, lower_bounds):
    bsz, seq, _ = x.shape
    tt = PROMPT_TILE
    nt = seq // tt
    cos_a, sin_a, cos_b, sin_b = tabs
    rt, ones_bd, bd_d, hg = consts
    tab_spec = pl.BlockSpec((tt, LANES), lambda b, t: (t, 0))
    in_arrays = [x, cos_a, sin_a, cos_b, sin_b] + [rt[n] for n in RET_TABLE_NAMES] + [
        hg[n] for n in HG_TABLE_NAMES] + [
        ones_bd, bd_d, w_in, w_br, w_out, g_pre, g_post, sink_col, wpool_bd, pscale, g_hgrn, lower_bounds]
    in_specs = [pl.BlockSpec((1, tt, D_MODEL), lambda b, t: (b, t, 0)), tab_spec, tab_spec, tab_spec, tab_spec]
    in_specs += [_const_spec(a.shape) for a in in_arrays[5:]]
    out_shape = (
        jax.ShapeDtypeStruct((bsz, seq, D_MODEL), F32),
        jax.ShapeDtypeStruct((bsz, H_A * DK_A, DV_A), F32),
        jax.ShapeDtypeStruct((bsz, WINDOW, KV_B * HD_B), F32),
        jax.ShapeDtypeStruct((bsz, WINDOW, KV_B * HD_B), F32),
        jax.ShapeDtypeStruct((bsz, POOL_HIST, W_C), F32),
        jax.ShapeDtypeStruct((bsz, H_D * DK_D, DV_D), F32),
    )
    out_specs = (
        pl.BlockSpec((1, tt, D_MODEL), lambda b, t: (b, t, 0)),
        pl.BlockSpec((1, H_A * DK_A, DV_A), lambda b, t: (b, 0, 0)),
        pl.BlockSpec((1, WINDOW, KV_B * HD_B), lambda b, t: (b, 0, 0)),
        pl.BlockSpec((1, WINDOW, KV_B * HD_B), lambda b, t: (b, 0, 0)),
        pl.BlockSpec((1, POOL_HIST, W_C), lambda b, t: (b, 0, 0)),
        pl.BlockSpec((1, H_D * DK_D, DV_D), lambda b, t: (b, 0, 0)),
    )
    scratch = [
        pltpu.VMEM((tt, D_MODEL), BF16),
        pltpu.VMEM((tt, W_MIX), BF16),
        pltpu.VMEM((H_A * DK_A, W_A), F32),
        pltpu.VMEM((W_D, H_D * DK_D), F32),
        pltpu.VMEM((WINDOW, KV_B * HD_B), F32),
        pltpu.VMEM((WINDOW, KV_B * HD_B), F32),
        pltpu.VMEM((16 + tt, W_C), F32),
    ]
    return pl.pallas_call(
        functools.partial(_prompt_layer_body, layer, tt),
        grid=(bsz, nt),
        in_specs=in_specs,
        out_specs=out_specs,
        out_shape=out_shape,
        scratch_shapes=scratch,
        compiler_params=pltpu.CompilerParams(
            dimension_semantics=("arbitrary", "arbitrary"), vmem_limit_bytes=VMEM_LIMIT),
        name=f"prompt_layer{layer}",
    )(*in_arrays)


def _sample_inproj_body(x_ref, gpre_ref, w_ref, o_ref):
    hb = (_rms(x_ref[...]) * gpre_ref[...]).astype(BF16)
    o_ref[...] = jnp.dot(hb, w_ref[...], preferred_element_type=F32)


def _sample_inproj(layer, x2, g_pre, w_in):
    n = x2.shape[0]
    cb = 512
    return pl.pallas_call(
        _sample_inproj_body,
        grid=(D_IN // cb,),
        in_specs=[pl.BlockSpec((n, D_MODEL), lambda j: (0, 0)),
                  pl.BlockSpec((1, D_MODEL), lambda j: (0, 0)),
                  pl.BlockSpec((D_MODEL, cb), lambda j: (0, j))],
        out_specs=pl.BlockSpec((n, cb), lambda j: (0, j)),
        out_shape=jax.ShapeDtypeStruct((n, D_IN), F32),
        compiler_params=pltpu.CompilerParams(dimension_semantics=("arbitrary",), vmem_limit_bytes=VMEM_LIMIT),
        name=f"sample_inproj{layer}",
    )(x2, g_pre, w_in)


SAMPLE_SEQS_PER_STEP = 8
SAMPLE_TABLE_NAMES = ("hmask", "dcat", "vmask", "qdec", "kdec", "cdec", "hm_a", "hm_d")


def _sample_tables(tq, bb):
    m = tq * bb
    lg = jnp.log1p(-jnp.power(2.0, -5.0 - jnp.arange(H_A, dtype=F32)))
    t = (jnp.arange(m) % tq).astype(F32)
    seq = jnp.arange(m) // tq
    rel = t[:, None] - t[None, :]
    same = (seq[:, None] == seq[None, :]) & (rel >= 0)
    dmask = jnp.where(same[None], jnp.exp(jnp.maximum(rel, 0.0)[None] * lg[:, None, None]), 0.0)
    dcat = jnp.transpose(dmask, (1, 0, 2)).reshape(m, H_A * m)
    head_of_row = jnp.repeat(jnp.arange(H_A), m)
    hmask = (head_of_row[:, None] == (jnp.arange(128) // DK_A)[None, :]).astype(F32)
    vmask = (head_of_row[:, None] == (jnp.arange(W_A) // DV_A)[None, :]).astype(F32)
    lg_lane = jnp.repeat(lg, DK_A)
    qdec = jnp.exp((t[:, None] + 1.0) * lg_lane[None, :])
    kdec = jnp.exp((tq - 1.0 - t)[:, None] * lg_lane[None, :])
    cdec = jnp.broadcast_to(jnp.exp(tq * lg_lane)[:, None], (H_A * DK_A, DV_A))
    head_of_small = jnp.repeat(jnp.arange(4), tq)
    hm_a = (head_of_small[:, None] == (jnp.arange(H_A * DK_A) // DK_A)[None, :]).astype(F32)
    hm_d = (head_of_small[:, None] == (jnp.arange(H_D * DK_D) // DK_D)[None, :]).astype(F32)
    return dict(hmask=hmask, dcat=dcat, vmask=vmask, qdec=qdec, kdec=kdec, cdec=cdec, hm_a=hm_a, hm_d=hm_d)


def _sample_mixer_body(layer, tq, bb, *refs):
    (p_ref, cosa_ref, sina_ref, cosb_ref, sinb_ref,
     hmask_ref, dcat_ref, vmask_ref, qdec_ref, kdec_ref, cdec_ref, hma_ref, hmd_ref,
     ones_ref, sink_ref, wpool_ref, pscale_ref, ghgrn_ref, lbnd_ref,
     sret_ref, ck_ref, cv_ref, hist_ref, shg_ref,
     y_ref, reto_ref, ko_ref, vo_ref, poolo_ref, hgo_ref) = refs
    ones_bd = ones_ref[...]
    cosa, sina, cosb, sinb = cosa_ref[...], sina_ref[...], cosb_ref[...], sinb_ref[...]
    lb = _lower_bound_row(lbnd_ref[...], layer)
    valid = _swa_valid(tq, 0)
    m = bb * tq
    p = p_ref[...]

    qa = _rope(p[:, QA:QA + 128], cosa, sina, DK_A // 2, DK_A)
    ka = _rope(p[:, KA:KA + 128], cosa, sina, DK_A // 2, DK_A) * (DK_A ** -0.5)
    va = p[:, VA:VA + W_A]
    kexp = jnp.concatenate([ka] * H_A, axis=0) * hmask_ref[...]
    sc = _dot_nt(qa, kexp) * dcat_ref[...]
    vexp = jnp.concatenate([va] * H_A, axis=0) * vmask_ref[...]
    oa_intra = _dot(sc, vexp)
    qa_dec = qa * qdec_ref[...]
    ka_dec = ka * kdec_ref[...]

    qb = _rope(p[:, QB:QB + W_B], jnp.concatenate([cosb] * 4, axis=1), jnp.concatenate([sinb] * 4, axis=1),
               ROT_B // 2, HD_B)
    kb = _rope(p[:, KB:KB + 128], cosb, sinb, ROT_B // 2, HD_B)
    vb = p[:, VB:VB + 128]

    chunk = math.gcd(tq, HGRN_CHUNK)
    fg = lb + (1.0 - lb) * _sigmoid(p[:, FD:FD + 256])
    qh = _silu(p[:, QD:QD + 256])
    kh = 1.0 - fg
    vh = p[:, ID:ID + 256]
    rc = lax.broadcasted_iota(jnp.int32, (m, W_D), 0) % chunk
    bcum = jnp.log(fg)
    sh = 1
    while sh < chunk:
        bcum = bcum + jnp.where(rc >= sh, _shift_rows(bcum, sh), 0.0)
        sh *= 2
    od_intra = _dot(qh * kh, ones_bd) * vh
    for j in range(1, chunk):
        a = jnp.where(rc >= j, qh * _shift_rows(kh, j) * jnp.exp(bcum - _shift_rows(bcum, j)), 0.0)
        od_intra = od_intra + _dot(a, ones_bd) * _shift_rows(vh, j)
    blast = bcum
    for j in range(1, chunk):
        blast = jnp.where(rc == chunk - 1 - j, pltpu.roll(bcum, m - j, 0), blast)
    qh_dec = qh * jnp.exp(bcum)
    kh_dec = kh * jnp.exp(blast - bcum)
    dech = jnp.exp(blast)
    d_hi = dech.astype(BF16)
    d_r1 = dech - d_hi.astype(F32)
    d_mid = d_r1.astype(BF16)
    d_lo = (d_r1 - d_mid.astype(F32)).astype(BF16)
    ones3 = jnp.ones((3, DV_D), BF16)

    hm_a = hma_ref[...]
    hm_d = hmd_ref[...]
    cdec = cdec_ref[...]

    def state_step(qd_b, kd_b, v_b, s, decmat, hm):
        r = _dot(jnp.concatenate([qd_b] * 4, axis=0) * hm, s)
        o = jnp.concatenate([r[h * tq:(h + 1) * tq] for h in range(4)], axis=1)
        vr = jnp.concatenate([v_b[:, h * 64:(h + 1) * 64] for h in range(4)], axis=0)
        s_new = decmat * s + _dot_tn(jnp.concatenate([kd_b] * 4, axis=0) * hm, vr)
        return o, s_new

    oa_inter, od_inter, ob, pooled = [], [], [], []
    for b in range(bb):
        r = slice(b * tq, (b + 1) * tq)

        o_b, s_new = state_step(qa_dec[r], ka_dec[r], va[r], sret_ref[0, b].reshape(H_A * DK_A, DV_A), cdec, hm_a)
        oa_inter.append(o_b)
        reto_ref[b] = s_new.reshape(H_A, DK_A, DV_A)

        k_prev, v_prev = ck_ref[0, b], cv_ref[0, b]
        ob.append(_swa_block(qb[r], kb[r], vb[r], [k_prev[:, 0:HD_B], k_prev[:, HD_B:]],
                             [v_prev[:, 0:HD_B], v_prev[:, HD_B:]], sink_ref[...], valid))
        ko_ref[b] = jnp.concatenate([k_prev[tq:], kb[r]], axis=0)
        vo_ref[b] = jnp.concatenate([v_prev[tq:], vb[r]], axis=0)

        ext = jnp.concatenate([jnp.zeros((1, W_C), F32), hist_ref[0, b], p[r, UC:UC + W_C]], axis=0)
        pooled.append(_pool_sums(ext, PAST_LEN))
        poolo_ref[b] = ext[tq + 1:tq + 16]

        r0 = b * tq
        dec3 = jnp.concatenate([d_hi[r0:r0 + 1], d_mid[r0:r0 + 1], d_lo[r0:r0 + 1]], axis=0)
        decmat = lax.dot_general(dec3, ones3, (((0,), (0,)), ((), ())), preferred_element_type=F32)
        o_b, st_new = state_step(qh_dec[r], kh_dec[r], vh[r], shg_ref[0, b].reshape(H_D * DK_D, DV_D),
                                 decmat, hm_d)
        od_inter.append(o_b)
        hgo_ref[b] = st_new.reshape(H_D, DK_D, DV_D)

    oa = oa_intra + jnp.concatenate(oa_inter, axis=0)
    oa = oa * lax.rsqrt(_group_mean_sq(oa, ones_bd) + EPS)
    ya = oa * _silu(p[:, ZA:ZA + W_A])
    yb = jnp.concatenate(ob, axis=0) * _silu(p[:, ZB:ZB + W_B])
    oc = _dot(jnp.concatenate(pooled, axis=0), wpool_ref[...]) * pscale_ref[...]
    yc = oc * _silu(p[:, ZC:ZC + W_C])
    od = od_intra + jnp.concatenate(od_inter, axis=0)
    od = od * lax.rsqrt(_group_mean_sq(od, ones_bd) + EPS) * ghgrn_ref[...]
    yd = od * _silu(p[:, ZD:ZD + W_D])
    y_ref[...] = jnp.concatenate([ya, yb, yc, yd], axis=1).astype(BF16)


def _sample_mixer(layer, proj, tq, tabs, consts, sink_col, wpool_bd, pscale, g_hgrn, lower_bounds,
                  s_ret, c_k, c_v, hist, s_hg):
    nb = proj.shape[0] // tq
    bb = SAMPLE_SEQS_PER_STEP
    cos_a, sin_a, cos_b, sin_b = tabs
    st, ones_bd = consts
    const_arrays = [cos_a, sin_a, cos_b, sin_b] + [st[n] for n in SAMPLE_TABLE_NAMES] + [
        ones_bd, sink_col, wpool_bd, pscale, g_hgrn, lower_bounds]

    def cspec(a):
        nd = a.ndim
        return pl.BlockSpec(a.shape, lambda i: (0,) * nd)

    def in_state_spec(a):
        nd = a.ndim
        return pl.BlockSpec((1, bb) + a.shape[2:], lambda i: (layer, i) + (0,) * (nd - 2))

    def out_state_spec(shape):
        nd = len(shape)
        return pl.BlockSpec((bb,) + shape[1:], lambda i: (i,) + (0,) * (nd - 1))

    state_arrays = [s_ret, c_k, c_v, hist, s_hg]
    state_out_shapes = [a.shape[1:] for a in state_arrays]
    out_shape = (jax.ShapeDtypeStruct((nb * tq, W_MIX), BF16),) + tuple(
        jax.ShapeDtypeStruct(s, F32) for s in state_out_shapes)
    return pl.pallas_call(
        functools.partial(_sample_mixer_body, layer, tq, bb),
        grid=(nb // bb,),
        in_specs=[pl.BlockSpec((bb * tq, GL), lambda i: (i, 0))] + [cspec(a) for a in const_arrays]
        + [in_state_spec(a) for a in state_arrays],
        out_specs=(pl.BlockSpec((bb * tq, W_MIX), lambda i: (i, 0)),) + tuple(
            out_state_spec(s) for s in state_out_shapes),
        out_shape=out_shape,
        compiler_params=pltpu.CompilerParams(dimension_semantics=("arbitrary",), vmem_limit_bytes=VMEM_LIMIT),
        name=f"sample_mixer{layer}",
    )(proj, *const_arrays, *state_arrays)


MERGE_COLS = 512


def _sample_merge_body(x_ref, y_ref, g0_ref, g1_ref, g2_ref, g3_ref, wbr_ref, wout_ref, gpost_ref, o_ref,
                       merged_ref):
    c = pl.program_id(0)
    merged = None
    for i, g_ref in enumerate((g0_ref, g1_ref, g2_ref, g3_ref)):
        term = _sigmoid(g_ref[...]) * jnp.dot(y_ref[:, Y_OFF[i]:Y_OFF[i] + BRANCH_WIDTHS[i]],
                                              wbr_ref[Y_OFF[i]:Y_OFF[i] + BRANCH_WIDTHS[i], :],
                                              preferred_element_type=F32)
        merged = term if merged is None else merged + term
    merged_ref[c] = merged.astype(BF16)

    @pl.when(c == pl.num_programs(0) - 1)
    def _():
        nchunk = D_MODEL // MERGE_COLS
        out = None
        for j in range(nchunk):
            part = jnp.dot(merged_ref[j], wout_ref[j * MERGE_COLS:(j + 1) * MERGE_COLS, :],
                           preferred_element_type=F32)
            out = part if out is None else out + part
        o_ref[...] = x_ref[...] + _rms(out) * gpost_ref[...]


def _sample_merge(layer, x2, y2, proj, w_br, w_out, g_post):
    n = x2.shape[0]
    nchunk = D_MODEL // MERGE_COLS
    gate_blk0 = GL // MERGE_COLS

    def gate_spec(i):
        return pl.BlockSpec((n, MERGE_COLS), lambda c: (0, gate_blk0 + i * nchunk + c))

    return pl.pallas_call(
        _sample_merge_body,
        grid=(nchunk,),
        in_specs=[pl.BlockSpec((n, D_MODEL), lambda c: (0, 0)),
                  pl.BlockSpec((n, W_MIX), lambda c: (0, 0)),
                  gate_spec(0), gate_spec(1), gate_spec(2), gate_spec(3),
                  pl.BlockSpec((W_MIX, MERGE_COLS), lambda c: (0, c)),
                  pl.BlockSpec((D_MODEL, D_MODEL), lambda c: (0, 0)),
                  pl.BlockSpec((1, D_MODEL), lambda c: (0, 0))],
        out_specs=pl.BlockSpec((n, D_MODEL), lambda c: (0, 0)),
        out_shape=jax.ShapeDtypeStruct(x2.shape, F32),
        scratch_shapes=[pltpu.VMEM((nchunk, n, MERGE_COLS), BF16)],
        compiler_params=pltpu.CompilerParams(dimension_semantics=("arbitrary",), vmem_limit_bytes=VMEM_LIMIT),
        name=f"sample_merge{layer}",
    )(x2, y2, proj, proj, proj, proj, w_br, w_out, g_post)


def kernel(x_prompt, x_sample, state_ret, cache_swa_k, cache_swa_v, state_pool, state_hgrn, w_in, w_branch,
           w_out, g_pre, g_post, attn_sink, w_pool, pool_scale, g_hgrn, lower_bounds):
    bsz, seq, _ = x_prompt.shape
    nb, tq, _ = x_sample.shape
    w_in_b = w_in.astype(BF16)
    w_br_b = w_branch.astype(BF16)
    w_out_b = w_out.astype(BF16)
    ones_bd = _block_ones_64().astype(BF16)
    bd_d = _block_ones_64().astype(F32)
    consts_p = (_ret_tables(RET_CHUNK), ones_bd, bd_d, _hgrn_tables())
    consts_s = (_sample_tables(tq, SAMPLE_SEQS_PER_STEP), ones_bd)
    tabs_p = _rope_tables(jnp.arange(seq))
    tabs_s = tuple(jnp.tile(a, (SAMPLE_SEQS_PER_STEP, 1)) for a in _rope_tables(PAST_LEN + jnp.arange(tq)))

    def sink_cols(sink_l, rows):
        return jnp.repeat(sink_l.reshape(KV_B, G_B), rows, axis=1)[..., None].astype(F32)

    cache_k4 = cache_swa_k.reshape(DEPTH, nb, WINDOW, KV_B * HD_B)
    cache_v4 = cache_swa_v.reshape(DEPTH, nb, WINDOW, KV_B * HD_B)
    xp = x_prompt
    xs = x_sample.reshape(nb * tq, D_MODEL)
    acc_p = [[] for _ in range(5)]
    acc_s = [[] for _ in range(5)]
    for l in range(DEPTH):
        wpool_bd = _pool_weight_bd(w_pool[l])
        pscale = pool_scale[l][None, :]
        ghg = g_hgrn[l][None, :]
        gpre = g_pre[l][None, :]
        gpost = g_post[l][None, :]

        xp, r_p, k_p, v_p, pool_p, h_p = _prompt_layer(
            l, xp, tabs_p, consts_p, w_in_b[l], w_br_b[l], w_out_b[l], gpre, gpost,
            sink_cols(attn_sink[l], WINDOW), wpool_bd, pscale, ghg, lower_bounds)
        for a, s in zip(acc_p, (r_p.reshape(bsz, H_A, DK_A, DV_A), k_p.reshape(bsz, WINDOW, KV_B, HD_B),
                                v_p.reshape(bsz, WINDOW, KV_B, HD_B), pool_p,
                                h_p.reshape(bsz, H_D, DK_D, DV_D))):
            a.append(s)

        proj = _sample_inproj(l, xs, gpre, w_in_b[l])
        y2, r_s, k_s, v_s, pool_s, h_s = _sample_mixer(
            l, proj, tq, tabs_s, consts_s, sink_cols(attn_sink[l], tq), wpool_bd, pscale, ghg, lower_bounds,
            state_ret, cache_k4, cache_v4, state_pool, state_hgrn)
        xs = _sample_merge(l, xs, y2, proj, w_br_b[l], w_out_b[l], gpost)
        for a, s in zip(acc_s, (r_s, k_s, v_s, pool_s, h_s)):
            a.append(s)

    outs_p = [jnp.stack(a) for a in acc_p]
    outs_s = [jnp.stack(a) for a in acc_s]
    outs_s[1] = outs_s[1].reshape(DEPTH, nb, WINDOW, KV_B, HD_B)
    outs_s[2] = outs_s[2].reshape(DEPTH, nb, WINDOW, KV_B, HD_B)
    return (xp, xs.reshape(nb, tq, D_MODEL), *outs_p, *outs_s)
```

```python
import functools
import math

import numpy as np
import jax
import jax.numpy as jnp
from jax import lax
from jax.experimental import pallas as pl
from jax.experimental.pallas import tpu as pltpu

F32 = jnp.float32
BF16 = jnp.bfloat16

D_MODEL = 1024
DEPTH = 4
PAST_LEN = 8192
H_A, DK_A, DV_A = 4, 32, 64
RET_CHUNK = 128
RET_THETA = 10000.0
H_B, KV_B, HD_B = 8, 2, 64
G_B = H_B // KV_B
WINDOW = 128
ROPE_THETA = 500000.0
ROT_B = HD_B // 4
W_C = 256
POOL_WINDOWS = (2, 4, 8, 16)
GC = 64
POOL_HIST = 15
H_D, DK_D, DV_D = 4, 64, 64
HGRN_CHUNK = 16
W_A, W_B, W_D = H_A * DV_A, H_B * HD_B, H_D * DV_D
BRANCH_WIDTHS = (W_A, W_B, W_C, W_D)
W_MIX = sum(BRANCH_WIDTHS)
EPS = 1e-6

QA, KA, VA, ZA = 0, 128, 256, 512
QB, KB, VB, ZB = 768, 1280, 1408, 1536
UC, ZC = 2048, 2304
QD, FD, ID, ZD = 2560, 2816, 3072, 3328
GL = 3584
D_IN = GL + 4 * D_MODEL
Y_OFF = (0, W_A, W_A + W_B, W_A + W_B + W_C)

LANES = 128
VMEM_LIMIT = 56 * 1024 * 1024
PROMPT_TILE = 256


def _dot(a, b):
    return jnp.dot(a.astype(BF16), b.astype(BF16), preferred_element_type=F32)


def _dot_nt(a, b):
    return lax.dot_general(a.astype(BF16), b.astype(BF16), (((1,), (1,)), ((), ())),
                           preferred_element_type=F32)


def _dot_tn(a, b):
    return lax.dot_general(a.astype(BF16), b.astype(BF16), (((0,), (0,)), ((), ())),
                           preferred_element_type=F32)


def _sigmoid(x):
    return 1.0 / (1.0 + jnp.exp(-x))


def _silu(x):
    return x * _sigmoid(x)


def _rms(x):
    return x * lax.rsqrt(jnp.mean(x * x, axis=-1, keepdims=True) + EPS)


def _shift_rows(x, j):
    m = x.shape[0]
    if m % 8 == 0:
        return pltpu.roll(x, j, 0)
    return jnp.concatenate([jnp.zeros((j, x.shape[1]), x.dtype), x[:m - j]], axis=0)


def _rope(x, cos, sin, half, group):
    n = x.shape[1]
    lane = lax.broadcasted_iota(jnp.int32, x.shape, 1)
    first = (lane % group) < half
    swapped = jnp.where(first, pltpu.roll(x, n - half, 1), pltpu.roll(x, half, 1))
    return x * cos + swapped * sin


def _group_mean_sq(o, ones_bd):
    sq = o * o
    hi = sq.astype(BF16)
    lo = (sq - hi.astype(F32)).astype(BF16)
    tot = (jnp.dot(hi, ones_bd, preferred_element_type=F32)
           + jnp.dot(lo, ones_bd, preferred_element_type=F32))
    return tot * (1.0 / 64.0)


def _retention_chunk(q, k, v, s_bd, t):
    kexp = jnp.concatenate([k] * H_A, axis=0) * t["hmask"]
    sc = _dot_nt(q, kexp) * t["dcat"]
    vexp = jnp.concatenate([v] * H_A, axis=0) * t["vmask"]
    lhs = jnp.concatenate([q * t["qdec"], sc], axis=1)
    rhs = jnp.concatenate([s_bd, vexp], axis=0)
    o = _dot(lhs, rhs)
    s_new = t["cdec"] * s_bd + t["bd_a"] * _dot_tn(k * t["kdec"], v)
    return o, s_new


def _swa_block(q, k_new, v_new, k_prev, v_prev, sink_col, valid):
    bq = q.shape[0]
    bias = jnp.where(valid, 0.0, -jnp.inf).astype(F32)
    bias = jnp.concatenate([bias] * G_B, axis=0)
    outs = []
    for kv in range(KV_B):
        lo = kv * HD_B
        qs = jnp.concatenate(
            [q[:, (kv * G_B + g) * HD_B:(kv * G_B + g + 1) * HD_B] for g in range(G_B)], axis=0)
        qs = qs * (HD_B ** -0.5)
        keys = jnp.concatenate([k_prev[kv], k_new[:, lo:lo + HD_B]], axis=0)
        vals = jnp.concatenate([v_prev[kv], v_new[:, lo:lo + HD_B]], axis=0)
        s = _dot_nt(qs, keys) + bias
        sk = sink_col[kv]
        m = jnp.maximum(jnp.max(s, axis=1, keepdims=True), sk)
        e = jnp.exp(s - m)
        vext = jnp.concatenate([vals, jnp.ones_like(vals)], axis=1)
        oe = _dot(e, vext)
        den = oe[:, HD_B:] + jnp.exp(sk - m)
        o = oe[:, :HD_B] / den
        outs.append(jnp.concatenate([o[g * bq:(g + 1) * bq] for g in range(G_B)], axis=1))
    return jnp.concatenate(outs, axis=1)


def _pool_tile(ext, pos0, wpool_bd, pool_scale):
    return _dot(_pool_sums(ext, pos0), wpool_bd) * pool_scale


def _pool_sums(ext, pos0):
    m = ext.shape[0] - 16
    s2 = ext[1:] + ext[:-1]
    s4 = s2[2:] + s2[:-2]
    s8 = s4[4:] + s4[:-4]
    s16 = s8[8:] + s8[:-8]
    u = ext[16:]
    lane = lax.broadcasted_iota(jnp.int32, (m, W_C), 1)
    grp = lane // GC
    win = jnp.where(grp == 0, s2[15:15 + m],
                    jnp.where(grp == 1, s4[13:13 + m],
                              jnp.where(grp == 2, s8[9:9 + m], s16[1:1 + m])))
    wlen = jnp.where(grp == 0, 2, jnp.where(grp == 1, 4, jnp.where(grp == 2, 8, 16)))
    pos = pos0 + lax.broadcasted_iota(jnp.int32, (m, W_C), 0)
    cnt = jnp.minimum(pos + 1, wlen).astype(F32)
    return win / cnt - u


HG_BLOCK = 128
HG_LEVELS = 7
HG_MXU_LEVELS = 3


def _hgrn_block_tree(q, k, v, lf, st_bd, t, ones_bd, bd_d):
    n = HG_BLOCK
    hi = lf.astype(BF16)
    r1 = lf - hi.astype(F32)
    mid = r1.astype(BF16)
    lo = (r1 - mid.astype(F32)).astype(BF16)
    amat = t["hg_a"]
    amat2 = jnp.concatenate([amat, amat], axis=1)
    ex = (jnp.dot(amat2, jnp.concatenate([hi, mid], axis=0), preferred_element_type=F32)
          + jnp.dot(amat, lo, preferred_element_type=F32))
    b = ex[0:n]
    blast = jnp.broadcast_to(b[n - 1:n], (n, W_D))
    qdec = q * jnp.exp(b)
    kdec = k * jnp.exp(blast - b)
    row = lax.broadcasted_iota(jnp.int32, (n, W_D), 0)
    lane = lax.broadcasted_iota(jnp.int32, (n, LANES), 1)
    lvl = t["hg_lvl"]
    xs = [jnp.zeros((n, n), F32) for _ in range(H_D)]
    for i in range(HG_LEVELS):
        s = 1 << i
        if i < HG_MXU_LEVELS:
            ex_s = ex[(1 + i) * n:(2 + i) * n]
        else:
            bmid = jnp.concatenate(
                [jnp.broadcast_to(b[j * 2 * s + s - 1:j * 2 * s + s], (2 * s, W_D)) for j in range(n // (2 * s))],
                axis=0)
            d = b - bmid
            ex_s = jnp.minimum(d, -d)
        second = (row // s) % 2 == 1
        p = jnp.where(second, q, k) * jnp.exp(ex_s)
        qt = jnp.where(second, p, 0.0).astype(BF16)
        kt = jnp.where(second, 0.0, p).astype(BF16)
        for pr in range(H_D // 2):
            c0 = pr * LANES
            ktp = kt[:, c0:c0 + LANES]
            zero = jnp.zeros((), BF16)
            kst = jnp.concatenate([jnp.where((lane // DK_D) == 0, ktp, zero),
                                   jnp.where((lane // DK_D) == 1, ktp, zero)], axis=0)
            x = lax.dot_general(qt[:, c0:c0 + LANES], kst, (((1,), (1,)), ((), ())),
                                preferred_element_type=F32)
            for hh in range(2):
                h = 2 * pr + hh
                xs[h] = jnp.where(lvl == i, x[:, hh * n:(hh + 1) * n], xs[h])
    vb = v.astype(BF16)
    vexp = jnp.concatenate([vb] * H_D, axis=0) * t["hg_vmask"]
    sc = jnp.concatenate(xs, axis=1).astype(BF16)
    o = jnp.dot(sc, vexp, preferred_element_type=F32)
    o = o + _dot(q * k, ones_bd) * v
    o = o + _dot_nt(qdec, st_bd)
    st_new = jnp.exp(b[n - 1:n]) * st_bd + bd_d * _dot_tn(v, kdec)
    return o, st_new


def _hgrn_tables():
    n = HG_BLOCK
    t = np.arange(n)[:, None]
    u = np.arange(n)[None, :]
    mats = [(u <= t)]
    for i in range(HG_MXU_LEVELS):
        s = 1 << i
        midp = (t // (2 * s)) * 2 * s + s - 1
        second = (t % (2 * s)) >= s
        mats.append(np.where(second, (u > midp) & (u <= t), (u > t) & (u <= midp)))
    amat = np.concatenate(mats, axis=0).astype(np.float32)
    x = np.bitwise_xor(t, u)
    lvl = np.where(u < t, np.floor(np.log2(np.maximum(x, 1))).astype(np.int32), -1).astype(np.int32)
    head_of_row = np.repeat(np.arange(H_D), n)
    vmask = (head_of_row[:, None] == (np.arange(W_D) // DV_D)[None, :]).astype(np.float32)
    return dict(hg_a=jnp.asarray(amat, BF16), hg_lvl=jnp.asarray(lvl), hg_vmask=jnp.asarray(vmask, BF16))


HG_TABLE_NAMES = ("hg_a", "hg_lvl", "hg_vmask")


def _lower_bound_row(lower_bounds, layer):
    x = lower_bounds
    mx = jnp.max(x, axis=0, keepdims=True)
    e = jnp.exp(x - mx)
    tot = jnp.sum(e, axis=0, keepdims=True)
    if layer == 0:
        return jnp.zeros_like(tot)
    acc = e[1:2]
    for i in range(2, layer + 1):
        acc = acc + e[i:i + 1]
    return acc / tot


def _swa_valid(bq, lim):
    row = lax.broadcasted_iota(jnp.int32, (bq, WINDOW + bq), 0)
    col = lax.broadcasted_iota(jnp.int32, (bq, WINDOW + bq), 1)
    ok = jnp.where(col >= row, jnp.where(col <= row + WINDOW, 1, 0), 0)
    ok = jnp.where(col >= lim, ok, 0)
    return ok > 0


def _rope_tables(pos):
    pos = pos.astype(F32)[:, None]
    half_a = DK_A // 2
    inv_a = jnp.power(RET_THETA, -jnp.arange(half_a, dtype=F32) / half_a)
    ang = pos * inv_a[None, :]
    ca, sa = lax.optimization_barrier((jnp.cos(ang), jnp.sin(ang)))
    cos_a = jnp.tile(jnp.concatenate([ca, ca], -1), (1, H_A))
    sin_a = jnp.tile(jnp.concatenate([-sa, sa], -1), (1, H_A))
    half_b = ROT_B // 2
    inv_b = jnp.power(ROPE_THETA, -jnp.arange(half_b, dtype=F32) / half_b)
    angb = pos * inv_b[None, :]
    t = pos.shape[0]
    one = jnp.ones((t, HD_B - ROT_B), F32)
    zero = jnp.zeros((t, HD_B - ROT_B), F32)
    cb, sb = lax.optimization_barrier((jnp.cos(angb), jnp.sin(angb)))
    cos_b = jnp.tile(jnp.concatenate([cb, cb, one], -1), (1, 2))
    sin_b = jnp.tile(jnp.concatenate([-sb, sb, zero], -1), (1, 2))
    return cos_a, sin_a, cos_b, sin_b


def _ret_tables(c):
    lg = jnp.log1p(-jnp.power(2.0, -5.0 - jnp.arange(H_A, dtype=F32)))
    idx = jnp.arange(c, dtype=F32)
    rel = idx[:, None] - idx[None, :]
    dmask = jnp.where(rel[None] >= 0, jnp.exp(jnp.maximum(rel, 0.0)[None] * lg[:, None, None]), 0.0)
    dcat = jnp.transpose(dmask, (1, 0, 2)).reshape(c, H_A * c)
    head_of_row = jnp.repeat(jnp.arange(H_A), c)
    hmask = (head_of_row[:, None] == (jnp.arange(128) // DK_A)[None, :]).astype(F32)
    vmask = (head_of_row[:, None] == (jnp.arange(W_A) // DV_A)[None, :]).astype(F32)
    lg_lane = jnp.repeat(lg, DK_A)
    qdec = jnp.exp((idx[:, None] + 1.0) * lg_lane[None, :])
    kdec = jnp.exp((c - 1.0 - idx)[:, None] * lg_lane[None, :])
    bd_a = ((jnp.arange(128) // DK_A)[:, None] == (jnp.arange(W_A) // DV_A)[None, :]).astype(F32)
    cdec = jnp.exp(c * lg_lane)[:, None] * bd_a
    return dict(hmask=hmask, dcat=dcat, vmask=vmask, qdec=qdec, kdec=kdec, cdec=cdec, bd_a=bd_a)


RET_TABLE_NAMES = ("hmask", "dcat", "vmask", "qdec", "kdec", "cdec", "bd_a")


def _block_ones_64():
    g = jnp.arange(256) // 64
    return (g[:, None] == g[None, :])


def _pool_weight_bd(w_pool_l):
    out = jnp.zeros((W_C, W_C), F32)
    for g in range(len(POOL_WINDOWS)):
        out = out.at[g * GC:(g + 1) * GC, g * GC:(g + 1) * GC].set(w_pool_l[g])
    return out.astype(BF16)


def _prompt_layer_body(layer, tt, *refs):
    (x_ref, cosa_ref, sina_ref, cosb_ref, sinb_ref,
     hmask_ref, dcat_ref, vmask_ref, qdec_ref, kdec_ref, cdec_ref, bda_ref,
     hga_ref, hglvl_ref, hgvm_ref,
     ones_ref, bdd_ref, win_ref, wbr_ref, wout_ref, gpre_ref, gpost_ref, sink_ref,
     wpool_ref, pscale_ref, ghgrn_ref, lbnd_ref,
     xo_ref, ret_ref, ko_ref, vo_ref, poolo_ref, hgo_ref,
     hb_ref, y_ref, sbd_ref, stbd_ref, kprev_ref, vprev_ref, ext_ref) = refs
    t_i = pl.program_id(1)
    n_t = pl.num_programs(1)

    @pl.when(t_i == 0)
    def _():
        sbd_ref[...] = jnp.zeros_like(sbd_ref)
        stbd_ref[...] = jnp.zeros_like(stbd_ref)
        kprev_ref[...] = jnp.zeros_like(kprev_ref)
        vprev_ref[...] = jnp.zeros_like(vprev_ref)
        ext_ref[0:16, :] = jnp.zeros((16, W_C), F32)

    x = x_ref[0]
    hb_ref[...] = (_rms(x) * gpre_ref[...]).astype(BF16)
    ones_bd = ones_ref[...]
    rt = dict(hmask=hmask_ref[...], dcat=dcat_ref[...], vmask=vmask_ref[...], qdec=qdec_ref[...],
              kdec=kdec_ref[...], cdec=cdec_ref[...], bd_a=bda_ref[...])
    nblk = tt // RET_CHUNK

    pa = jnp.dot(hb_ref[...], win_ref[:, QA:ZA + W_A], preferred_element_type=F32)
    qa = _rope(pa[:, QA:QA + 128], cosa_ref[...], sina_ref[...], DK_A // 2, DK_A)
    ka = _rope(pa[:, KA:KA + 128], cosa_ref[...], sina_ref[...], DK_A // 2, DK_A) * (DK_A ** -0.5)
    va = pa[:, VA:VA + W_A]
    s_bd = sbd_ref[...]
    oa = []
    for c in range(nblk):
        r = slice(c * RET_CHUNK, (c + 1) * RET_CHUNK)
        o_c, s_bd = _retention_chunk(qa[r], ka[r], va[r], s_bd, rt)
        oa.append(o_c)
    sbd_ref[...] = s_bd
    oa = jnp.concatenate(oa, axis=0)
    oa = oa * lax.rsqrt(_group_mean_sq(oa, ones_bd) + EPS)
    y_ref[:, Y_OFF[0]:Y_OFF[0] + W_A] = (oa * _silu(pa[:, ZA:ZA + W_A])).astype(BF16)

    pb = jnp.dot(hb_ref[...], win_ref[:, QB:ZB + W_B], preferred_element_type=F32)
    cosb, sinb = cosb_ref[...], sinb_ref[...]
    qb = _rope(pb[:, 0:W_B], jnp.concatenate([cosb] * 4, axis=1), jnp.concatenate([sinb] * 4, axis=1),
               ROT_B // 2, HD_B)
    kb = _rope(pb[:, KB - QB:KB - QB + 128], cosb, sinb, ROT_B // 2, HD_B)
    vb = pb[:, VB - QB:VB - QB + 128]
    sink_col = sink_ref[...]
    ob = []
    k_prev, v_prev = kprev_ref[...], vprev_ref[...]
    for c in range(nblk):
        r = slice(c * WINDOW, (c + 1) * WINDOW)
        lim = jnp.where(t_i == 0, WINDOW, 0) if c == 0 else 0
        valid = _swa_valid(WINDOW, lim)
        ob.append(_swa_block(qb[r], kb[r], vb[r], [k_prev[:, 0:HD_B], k_prev[:, HD_B:]],
                             [v_prev[:, 0:HD_B], v_prev[:, HD_B:]], sink_col, valid))
        k_prev, v_prev = kb[r], vb[r]
    kprev_ref[...] = k_prev
    vprev_ref[...] = v_prev
    ob = jnp.concatenate(ob, axis=0)
    y_ref[:, Y_OFF[1]:Y_OFF[1] + W_B] = (ob * _silu(pb[:, ZB - QB:ZB - QB + W_B])).astype(BF16)

    pc = jnp.dot(hb_ref[...], win_ref[:, UC:ZC + W_C], preferred_element_type=F32)
    ext_ref[16:16 + tt, :] = pc[:, 0:W_C]
    oc = _pool_tile(ext_ref[...], t_i * tt, wpool_ref[...], pscale_ref[...])
    y_ref[:, Y_OFF[2]:Y_OFF[2] + W_C] = (oc * _silu(pc[:, W_C:2 * W_C])).astype(BF16)

    ext_ref[0:16, :] = ext_ref[tt:tt + 16, :]

    pd = jnp.dot(hb_ref[...], win_ref[:, QD:ZD + W_D], preferred_element_type=F32)
    lb = _lower_bound_row(lbnd_ref[...], layer)
    ht = dict(hg_a=hga_ref[...], hg_lvl=hglvl_ref[...], hg_vmask=hgvm_ref[...])
    fg = lb + (1.0 - lb) * _sigmoid(pd[:, 256:512])
    qh = _silu(pd[:, 0:256])
    kh = 1.0 - fg
    lfh = jnp.log(fg)
    vh = pd[:, 512:768]
    st_bd = stbd_ref[...]
    od = []
    for c in range(tt // HG_BLOCK):
        r = slice(c * HG_BLOCK, (c + 1) * HG_BLOCK)
        o_c, st_bd = _hgrn_block_tree(qh[r], kh[r], vh[r], lfh[r], st_bd, ht, ones_bd, bdd_ref[...])
        od.append(o_c)
    od = jnp.concatenate(od, axis=0)
    stbd_ref[...] = st_bd
    od = od * lax.rsqrt(_group_mean_sq(od, ones_bd) + EPS) * ghgrn_ref[...]
    y_ref[:, Y_OFF[3]:Y_OFF[3] + W_D] = (od * _silu(pd[:, 768:1024])).astype(BF16)

    merged = None
    for i in range(4):
        g = _sigmoid(jnp.dot(hb_ref[...], win_ref[:, GL + i * D_MODEL:GL + (i + 1) * D_MODEL],
                             preferred_element_type=F32))
        term = g * jnp.dot(y_ref[:, Y_OFF[i]:Y_OFF[i] + BRANCH_WIDTHS[i]],
                           wbr_ref[Y_OFF[i]:Y_OFF[i] + BRANCH_WIDTHS[i], :], preferred_element_type=F32)
        merged = term if merged is None else merged + term
    out = jnp.dot(merged.astype(BF16), wout_ref[...], preferred_element_type=F32)
    xo_ref[0] = x + _rms(out) * gpost_ref[...]

    @pl.when(t_i == n_t - 1)
    def _():
        s = sbd_ref[...]
        ret_ref[0] = s[:, 0:64] + s[:, 64:128] + s[:, 128:192] + s[:, 192:256]
        st = stbd_ref[...]
        hgo_ref[0] = (st[0:64] + st[64:128] + st[128:192] + st[192:256]).T
        ko_ref[0] = kprev_ref[...]
        vo_ref[0] = vprev_ref[...]
        poolo_ref[0] = ext_ref[pl.ds(1, POOL_HIST), :]


def _const_spec(shape):
    nd = len(shape)
    return pl.BlockSpec(shape, lambda b, t: (0,) * nd, pipeline_mode=pl.Buffered(1))


def _prompt_layer(layer, x, tabs, consts, w_in, w_br, w_out, g_pre, g_post, sink_col, wpool_bd, pscale,
                  g_hgrn, lower_bounds):
    bsz, seq, _ = x.shape
    tt = PROMPT_TILE
    nt = seq // tt
    cos_a, sin_a, cos_b, sin_b = tabs
    rt, ones_bd, bd_d, hg = consts
    tab_spec = pl.BlockSpec((tt, LANES), lambda b, t: (t, 0))
    in_arrays = [x, cos_a, sin_a, cos_b, sin_b] + [rt[n] for n in RET_TABLE_NAMES] + [
        hg[n] for n in HG_TABLE_NAMES] + [
        ones_bd, bd_d, w_in, w_br, w_out, g_pre, g_post, sink_col, wpool_bd, pscale, g_hgrn, lower_bounds]
    in_specs = [pl.BlockSpec((1, tt, D_MODEL), lambda b, t: (b, t, 0)), tab_spec, tab_spec, tab_spec, tab_spec]
    in_specs += [_const_spec(a.shape) for a in in_arrays[5:]]
    out_shape = (
        jax.ShapeDtypeStruct((bsz, seq, D_MODEL), F32),
        jax.ShapeDtypeStruct((bsz, H_A * DK_A, DV_A), F32),
        jax.ShapeDtypeStruct((bsz, WINDOW, KV_B * HD_B), F32),
        jax.ShapeDtypeStruct((bsz, WINDOW, KV_B * HD_B), F32),
        jax.ShapeDtypeStruct((bsz, POOL_HIST, W_C), F32),
        jax.ShapeDtypeStruct((bsz, H_D * DK_D, DV_D), F32),
    )
    out_specs = (
        pl.BlockSpec((1, tt, D_MODEL), lambda b, t: (b, t, 0)),
        pl.BlockSpec((1, H_A * DK_A, DV_A), lambda b, t: (b, 0, 0)),
        pl.BlockSpec((1, WINDOW, KV_B * HD_B), lambda b, t: (b, 0, 0)),
        pl.BlockSpec((1, WINDOW, KV_B * HD_B), lambda b, t: (b, 0, 0)),
        pl.BlockSpec((1, POOL_HIST, W_C), lambda b, t: (b, 0, 0)),
        pl.BlockSpec((1, H_D * DK_D, DV_D), lambda b, t: (b, 0, 0)),
    )
    scratch = [
        pltpu.VMEM((tt, D_MODEL), BF16),
        pltpu.VMEM((tt, W_MIX), BF16),
        pltpu.VMEM((H_A * DK_A, W_A), F32),
        pltpu.VMEM((W_D, H_D * DK_D), F32),
        pltpu.VMEM((WINDOW, KV_B * HD_B), F32),
        pltpu.VMEM((WINDOW, KV_B * HD_B), F32),
        pltpu.VMEM((16 + tt, W_C), F32),
    ]
    return pl.pallas_call(
        functools.partial(_prompt_layer_body, layer, tt),
        grid=(bsz, nt),
        in_specs=in_specs,
        out_specs=out_specs,
        out_shape=out_shape,
        scratch_shapes=scratch,
        compiler_params=pltpu.CompilerParams(
            dimension_semantics=("arbitrary", "arbitrary"), vmem_limit_bytes=VMEM_LIMIT),
        name=f"prompt_layer{layer}",
    )(*in_arrays)


def _sample_inproj_body(x_ref, gpre_ref, w_ref, o_ref):
    hb = (_rms(x_ref[...]) * gpre_ref[...]).astype(BF16)
    o_ref[...] = jnp.dot(hb, w_ref[...], preferred_element_type=F32)


def _sample_inproj(layer, x2, g_pre, w_in):
    n = x2.shape[0]
    cb = 512
    return pl.pallas_call(
        _sample_inproj_body,
        grid=(D_IN // cb,),
        in_specs=[pl.BlockSpec((n, D_MODEL), lambda j: (0, 0)),
                  pl.BlockSpec((1, D_MODEL), lambda j: (0, 0)),
                  pl.BlockSpec((D_MODEL, cb), lambda j: (0, j))],
        out_specs=pl.BlockSpec((n, cb), lambda j: (0, j)),
        out_shape=jax.ShapeDtypeStruct((n, D_IN), F32),
        compiler_params=pltpu.CompilerParams(dimension_semantics=("arbitrary",), vmem_limit_bytes=VMEM_LIMIT),
        name=f"sample_inproj{layer}",
    )(x2, g_pre, w_in)


SAMPLE_SEQS_PER_STEP = 8
SAMPLE_TABLE_NAMES = ("hmask", "dcat", "vmask", "qdec", "kdec", "cdec", "hm_a", "hm_d")


def _sample_tables(tq, bb):
    m = tq * bb
    lg = jnp.log1p(-jnp.power(2.0, -5.0 - jnp.arange(H_A, dtype=F32)))
    t = (jnp.arange(m) % tq).astype(F32)
    seq = jnp.arange(m) // tq
    rel = t[:, None] - t[None, :]
    same = (seq[:, None] == seq[None, :]) & (rel >= 0)
    dmask = jnp.where(same[None], jnp.exp(jnp.maximum(rel, 0.0)[None] * lg[:, None, None]), 0.0)
    dcat = jnp.transpose(dmask, (1, 0, 2)).reshape(m, H_A * m)
    head_of_row = jnp.repeat(jnp.arange(H_A), m)
    hmask = (head_of_row[:, None] == (jnp.arange(128) // DK_A)[None, :]).astype(F32)
    vmask = (head_of_row[:, None] == (jnp.arange(W_A) // DV_A)[None, :]).astype(F32)
    lg_lane = jnp.repeat(lg, DK_A)
    qdec = jnp.exp((t[:, None] + 1.0) * lg_lane[None, :])
    kdec = jnp.exp((tq - 1.0 - t)[:, None] * lg_lane[None, :])
    cdec = jnp.broadcast_to(jnp.exp(tq * lg_lane)[:, None], (H_A * DK_A, DV_A))
    head_of_small = jnp.repeat(jnp.arange(4), tq)
    hm_a = (head_of_small[:, None] == (jnp.arange(H_A * DK_A) // DK_A)[None, :]).astype(F32)
    hm_d = (head_of_small[:, None] == (jnp.arange(H_D * DK_D) // DK_D)[None, :]).astype(F32)
    return dict(hmask=hmask, dcat=dcat, vmask=vmask, qdec=qdec, kdec=kdec, cdec=cdec, hm_a=hm_a, hm_d=hm_d)


def _sample_mixer_body(layer, tq, bb, *refs):
    (p_ref, cosa_ref, sina_ref, cosb_ref, sinb_ref,
     hmask_ref, dcat_ref, vmask_ref, qdec_ref, kdec_ref, cdec_ref, hma_ref, hmd_ref,
     ones_ref, sink_ref, wpool_ref, pscale_ref, ghgrn_ref, lbnd_ref,
     sret_ref, ck_ref, cv_ref, hist_ref, shg_ref,
     y_ref, reto_ref, ko_ref, vo_ref, poolo_ref, hgo_ref) = refs
    ones_bd = ones_ref[...]
    cosa, sina, cosb, sinb = cosa_ref[...], sina_ref[...], cosb_ref[...], sinb_ref[...]
    lb = _lower_bound_row(lbnd_ref[...], layer)
    valid = _swa_valid(tq, 0)
    m = bb * tq
    p = p_ref[...]

    qa = _rope(p[:, QA:QA + 128], cosa, sina, DK_A // 2, DK_A)
    ka = _rope(p[:, KA:KA + 128], cosa, sina, DK_A // 2, DK_A) * (DK_A ** -0.5)
    va = p[:, VA:VA + W_A]
    kexp = jnp.concatenate([ka] * H_A, axis=0) * hmask_ref[...]
    sc = _dot_nt(qa, kexp) * dcat_ref[...]
    vexp = jnp.concatenate([va] * H_A, axis=0) * vmask_ref[...]
    oa_intra = _dot(sc, vexp)
    qa_dec = qa * qdec_ref[...]
    ka_dec = ka * kdec_ref[...]

    qb = _rope(p[:, QB:QB + W_B], jnp.concatenate([cosb] * 4, axis=1), jnp.concatenate([sinb] * 4, axis=1),
               ROT_B // 2, HD_B)
    kb = _rope(p[:, KB:KB + 128], cosb, sinb, ROT_B // 2, HD_B)
    vb = p[:, VB:VB + 128]

    chunk = math.gcd(tq, HGRN_CHUNK)
    fg = lb + (1.0 - lb) * _sigmoid(p[:, FD:FD + 256])
    qh = _silu(p[:, QD:QD + 256])
    kh = 1.0 - fg
    vh = p[:, ID:ID + 256]
    rc = lax.broadcasted_iota(jnp.int32, (m, W_D), 0) % chunk
    bcum = jnp.log(fg)
    sh = 1
    while sh < chunk:
        bcum = bcum + jnp.where(rc >= sh, _shift_rows(bcum, sh), 0.0)
        sh *= 2
    od_intra = _dot(qh * kh, ones_bd) * vh
    for j in range(1, chunk):
        a = jnp.where(rc >= j, qh * _shift_rows(kh, j) * jnp.exp(bcum - _shift_rows(bcum, j)), 0.0)
        od_intra = od_intra + _dot(a, ones_bd) * _shift_rows(vh, j)
    blast = bcum
    for j in range(1, chunk):
        blast = jnp.where(rc == chunk - 1 - j, pltpu.roll(bcum, m - j, 0), blast)
    qh_dec = qh * jnp.exp(bcum)
    kh_dec = kh * jnp.exp(blast - bcum)
    dech = jnp.exp(blast)
    d_hi = dech.astype(BF16)
    d_r1 = dech - d_hi.astype(F32)
    d_mid = d_r1.astype(BF16)
    d_lo = (d_r1 - d_mid.astype(F32)).astype(BF16)
    ones3 = jnp.ones((3, DV_D), BF16)

    hm_a = hma_ref[...]
    hm_d = hmd_ref[...]
    cdec = cdec_ref[...]

    def state_step(qd_b, kd_b, v_b, s, decmat, hm):
        r = _dot(jnp.concatenate([qd_b] * 4, axis=0) * hm, s)
        o = jnp.concatenate([r[h * tq:(h + 1) * tq] for h in range(4)], axis=1)
        vr = jnp.concatenate([v_b[:, h * 64:(h + 1) * 64] for h in range(4)], axis=0)
        s_new = decmat * s + _dot_tn(jnp.concatenate([kd_b] * 4, axis=0) * hm, vr)
        return o, s_new

    oa_inter, od_inter, ob, pooled = [], [], [], []
    for b in range(bb):
        r = slice(b * tq, (b + 1) * tq)

        o_b, s_new = state_step(qa_dec[r], ka_dec[r], va[r], sret_ref[0, b].reshape(H_A * DK_A, DV_A), cdec, hm_a)
        oa_inter.append(o_b)
        reto_ref[b] = s_new.reshape(H_A, DK_A, DV_A)

        k_prev, v_prev = ck_ref[0, b], cv_ref[0, b]
        ob.append(_swa_block(qb[r], kb[r], vb[r], [k_prev[:, 0:HD_B], k_prev[:, HD_B:]],
                             [v_prev[:, 0:HD_B], v_prev[:, HD_B:]], sink_ref[...], valid))
        ko_ref[b] = jnp.concatenate([k_prev[tq:], kb[r]], axis=0)
        vo_ref[b] = jnp.concatenate([v_prev[tq:], vb[r]], axis=0)

        ext = jnp.concatenate([jnp.zeros((1, W_C), F32), hist_ref[0, b], p[r, UC:UC + W_C]], axis=0)
        pooled.append(_pool_sums(ext, PAST_LEN))
        poolo_ref[b] = ext[tq + 1:tq + 16]

        r0 = b * tq
        dec3 = jnp.concatenate([d_hi[r0:r0 + 1], d_mid[r0:r0 + 1], d_lo[r0:r0 + 1]], axis=0)
        decmat = lax.dot_general(dec3, ones3, (((0,), (0,)), ((), ())), preferred_element_type=F32)
        o_b, st_new = state_step(qh_dec[r], kh_dec[r], vh[r], shg_ref[0, b].reshape(H_D * DK_D, DV_D),
                                 decmat, hm_d)
        od_inter.append(o_b)
        hgo_ref[b] = st_new.reshape(H_D, DK_D, DV_D)

    oa = oa_intra + jnp.concatenate(oa_inter, axis=0)
    oa = oa * lax.rsqrt(_group_mean_sq(oa, ones_bd) + EPS)
    ya = oa * _silu(p[:, ZA:ZA + W_A])
    yb = jnp.concatenate(ob, axis=0) * _silu(p[:, ZB:ZB + W_B])
    oc = _dot(jnp.concatenate(pooled, axis=0), wpool_ref[...]) * pscale_ref[...]
    yc = oc * _silu(p[:, ZC:ZC + W_C])
    od = od_intra + jnp.concatenate(od_inter, axis=0)
    od = od * lax.rsqrt(_group_mean_sq(od, ones_bd) + EPS) * ghgrn_ref[...]
    yd = od * _silu(p[:, ZD:ZD + W_D])
    y_ref[...] = jnp.concatenate([ya, yb, yc, yd], axis=1).astype(BF16)


def _sample_mixer(layer, proj, tq, tabs, consts, sink_col, wpool_bd, pscale, g_hgrn, lower_bounds,
                  s_ret, c_k, c_v, hist, s_hg):
    nb = proj.shape[0] // tq
    bb = SAMPLE_SEQS_PER_STEP
    cos_a, sin_a, cos_b, sin_b = tabs
    st, ones_bd = consts
    const_arrays = [cos_a, sin_a, cos_b, sin_b] + [st[n] for n in SAMPLE_TABLE_NAMES] + [
        ones_bd, sink_col, wpool_bd, pscale, g_hgrn, lower_bounds]

    def cspec(a):
        nd = a.ndim
        return pl.BlockSpec(a.shape, lambda i: (0,) * nd)

    def in_state_spec(a):
        nd = a.ndim
        return pl.BlockSpec((1, bb) + a.shape[2:], lambda i: (layer, i) + (0,) * (nd - 2))

    def out_state_spec(shape):
        nd = len(shape)
        return pl.BlockSpec((bb,) + shape[1:], lambda i: (i,) + (0,) * (nd - 1))

    state_arrays = [s_ret, c_k, c_v, hist, s_hg]
    state_out_shapes = [a.shape[1:] for a in state_arrays]
    out_shape = (jax.ShapeDtypeStruct((nb * tq, W_MIX), BF16),) + tuple(
        jax.ShapeDtypeStruct(s, F32) for s in state_out_shapes)
    return pl.pallas_call(
        functools.partial(_sample_mixer_body, layer, tq, bb),
        grid=(nb // bb,),
        in_specs=[pl.BlockSpec((bb * tq, GL), lambda i: (i, 0))] + [cspec(a) for a in const_arrays]
        + [in_state_spec(a) for a in state_arrays],
        out_specs=(pl.BlockSpec((bb * tq, W_MIX), lambda i: (i, 0)),) + tuple(
            out_state_spec(s) for s in state_out_shapes),
        out_shape=out_shape,
        compiler_params=pltpu.CompilerParams(dimension_semantics=("arbitrary",), vmem_limit_bytes=VMEM_LIMIT),
        name=f"sample_mixer{layer}",
    )(proj, *const_arrays, *state_arrays)


MERGE_COLS = 512


def _sample_merge_body(x_ref, y_ref, g0_ref, g1_ref, g2_ref, g3_ref, wbr_ref, wout_ref, gpost_ref, o_ref,
                       merged_ref):
    c = pl.program_id(0)
    merged = None
    for i, g_ref in enumerate((g0_ref, g1_ref, g2_ref, g3_ref)):
        term = _sigmoid(g_ref[...]) * jnp.dot(y_ref[:, Y_OFF[i]:Y_OFF[i] + BRANCH_WIDTHS[i]],
                                              wbr_ref[Y_OFF[i]:Y_OFF[i] + BRANCH_WIDTHS[i], :],
                                              preferred_element_type=F32)
        merged = term if merged is None else merged + term
    merged_ref[c] = merged.astype(BF16)

    @pl.when(c == pl.num_programs(0) - 1)
    def _():
        nchunk = D_MODEL // MERGE_COLS
        out = None
        for j in range(nchunk):
            part = jnp.dot(merged_ref[j], wout_ref[j * MERGE_COLS:(j + 1) * MERGE_COLS, :],
                           preferred_element_type=F32)
            out = part if out is None else out + part
        o_ref[...] = x_ref[...] + _rms(out) * gpost_ref[...]


def _sample_merge(layer, x2, y2, proj, w_br, w_out, g_post):
    n = x2.shape[0]
    nchunk = D_MODEL // MERGE_COLS
    gate_blk0 = GL // MERGE_COLS

    def gate_spec(i):
        return pl.BlockSpec((n, MERGE_COLS), lambda c: (0, gate_blk0 + i * nchunk + c))

    return pl.pallas_call(
        _sample_merge_body,
        grid=(nchunk,),
        in_specs=[pl.BlockSpec((n, D_MODEL), lambda c: (0, 0)),
                  pl.BlockSpec((n, W_MIX), lambda c: (0, 0)),
                  gate_spec(0), gate_spec(1), gate_spec(2), gate_spec(3),
                  pl.BlockSpec((W_MIX, MERGE_COLS), lambda c: (0, c)),
                  pl.BlockSpec((D_MODEL, D_MODEL), lambda c: (0, 0)),
                  pl.BlockSpec((1, D_MODEL), lambda c: (0, 0))],
        out_specs=pl.BlockSpec((n, D_MODEL), lambda c: (0, 0)),
        out_shape=jax.ShapeDtypeStruct(x2.shape, F32),
        scratch_shapes=[pltpu.VMEM((nchunk, n, MERGE_COLS), BF16)],
        compiler_params=pltpu.CompilerParams(dimension_semantics=("arbitrary",), vmem_limit_bytes=VMEM_LIMIT),
        name=f"sample_merge{layer}",
    )(x2, y2, proj, proj, proj, proj, w_br, w_out, g_post)


def kernel(x_prompt, x_sample, state_ret, cache_swa_k, cache_swa_v, state_pool, state_hgrn, w_in, w_branch,
           w_out, g_pre, g_post, attn_sink, w_pool, pool_scale, g_hgrn, lower_bounds):
    bsz, seq, _ = x_prompt.shape
    nb, tq, _ = x_sample.shape
    w_in_b = w_in.astype(BF16)
    w_br_b = w_branch.astype(BF16)
    w_out_b = w_out.astype(BF16)
    ones_bd = _block_ones_64().astype(BF16)
    bd_d = _block_ones_64().astype(F32)
    consts_p = (_ret_tables(RET_CHUNK), ones_bd, bd_d, _hgrn_tables())
    consts_s = (_sample_tables(tq, SAMPLE_SEQS_PER_STEP), ones_bd)
    tabs_p = _rope_tables(jnp.arange(seq))
    tabs_s = tuple(jnp.tile(a, (SAMPLE_SEQS_PER_STEP, 1)) for a in _rope_tables(PAST_LEN + jnp.arange(tq)))

    def sink_cols(sink_l, rows):
        return jnp.repeat(sink_l.reshape(KV_B, G_B), rows, axis=1)[..., None].astype(F32)

    cache_k4 = cache_swa_k.reshape(DEPTH, nb, WINDOW, KV_B * HD_B)
    cache_v4 = cache_swa_v.reshape(DEPTH, nb, WINDOW, KV_B * HD_B)
    xp = x_prompt
    xs = x_sample.reshape(nb * tq, D_MODEL)
    acc_p = [[] for _ in range(5)]
    acc_s = [[] for _ in range(5)]
    for l in range(DEPTH):
        wpool_bd = _pool_weight_bd(w_pool[l])
        pscale = pool_scale[l][None, :]
        ghg = g_hgrn[l][None, :]
        gpre = g_pre[l][None, :]
        gpost = g_post[l][None, :]

        xp, r_p, k_p, v_p, pool_p, h_p = _prompt_layer(
            l, xp, tabs_p, consts_p, w_in_b[l], w_br_b[l], w_out_b[l], gpre, gpost,
            sink_cols(attn_sink[l], WINDOW), wpool_bd, pscale, ghg, lower_bounds)
        for a, s in zip(acc_p, (r_p.reshape(bsz, H_A, DK_A, DV_A), k_p.reshape(bsz, WINDOW, KV_B, HD_B),
                                v_p.reshape(bsz, WINDOW, KV_B, HD_B), pool_p,
                                h_p.reshape(bsz, H_D, DK_D, DV_D))):
            a.append(s)

        proj = _sample_inproj(l, xs, gpre, w_in_b[l])
        y2, r_s, k_s, v_s, pool_s, h_s = _sample_mixer(
            l, proj, tq, tabs_s, consts_s, sink_cols(attn_sink[l], tq), wpool_bd, pscale, ghg, lower_bounds,
            state_ret, cache_k4, cache_v4, state_pool, state_hgrn)
        xs = _sample_merge(l, xs, y2, proj, w_br_b[l], w_out_b[l], gpost)
        for a, s in zip(acc_s, (r_s, k_s, v_s, pool_s, h_s)):
            a.append(s)

    outs_p = [jnp.stack(a) for a in acc_p]
    outs_s = [jnp.stack(a) for a in acc_s]
    outs_s[1] = outs_s[1].reshape(DEPTH, nb, WINDOW, KV_B, HD_B)
    outs_s[2] = outs_s[2].reshape(DEPTH, nb, WINDOW, KV_B, HD_B)
    return (xp, xs.reshape(nb, tq, D_MODEL), *outs_p, *outs_s)
```

```python
import functools
import math

import numpy as np
import jax
import jax.numpy as jnp
from jax import lax
from jax.experimental import pallas as pl
from jax.experimental.pallas import tpu as pltpu

F32 = jnp.float32
BF16 = jnp.bfloat16

D_MODEL = 1024
DEPTH = 4
PAST_LEN = 8192
H_A, DK_A, DV_A = 4, 32, 64
RET_CHUNK = 128
RET_THETA = 10000.0
H_B, KV_B, HD_B = 8, 2, 64
G_B = H_B // KV_B
WINDOW = 128
ROPE_THETA = 500000.0
ROT_B = HD_B // 4
W_C = 256
POOL_WINDOWS = (2, 4, 8, 16)
GC = 64
POOL_HIST = 15
H_D, DK_D, DV_D = 4, 64, 64
HGRN_CHUNK = 16
W_A, W_B, W_D = H_A * DV_A, H_B * HD_B, H_D * DV_D
BRANCH_WIDTHS = (W_A, W_B, W_C, W_D)
W_MIX = sum(BRANCH_WIDTHS)
EPS = 1e-6

QA, KA, VA, ZA = 0, 128, 256, 512
QB, KB, VB, ZB = 768, 1280, 1408, 1536
UC, ZC = 2048, 2304
QD, FD, ID, ZD = 2560, 2816, 3072, 3328
GL = 3584
D_IN = GL + 4 * D_MODEL
Y_OFF = (0, W_A, W_A + W_B, W_A + W_B + W_C)

LANES = 128
VMEM_LIMIT = 56 * 1024 * 1024
PROMPT_TILE = 256


def _dot(a, b):
    return jnp.dot(a.astype(BF16), b.astype(BF16), preferred_element_type=F32)


def _dot_nt(a, b):
    return lax.dot_general(a.astype(BF16), b.astype(BF16), (((1,), (1,)), ((), ())),
                           preferred_element_type=F32)


def _dot_tn(a, b):
    return lax.dot_general(a.astype(BF16), b.astype(BF16), (((0,), (0,)), ((), ())),
                           preferred_element_type=F32)


def _sigmoid(x):
    return 1.0 / (1.0 + jnp.exp(-x))


def _silu(x):
    return x * _sigmoid(x)


def _rms(x):
    return x * lax.rsqrt(jnp.mean(x * x, axis=-1, keepdims=True) + EPS)


def _shift_rows(x, j):
    m = x.shape[0]
    if m % 8 == 0:
        return pltpu.roll(x, j, 0)
    return jnp.concatenate([jnp.zeros((j, x.shape[1]), x.dtype), x[:m - j]], axis=0)


def _rope(x, cos, sin, half, group):
    n = x.shape[1]
    lane = lax.broadcasted_iota(jnp.int32, x.shape, 1)
    first = (lane % group) < half
    swapped = jnp.where(first, pltpu.roll(x, n - half, 1), pltpu.roll(x, half, 1))
    return x * cos + swapped * sin


def _group_mean_sq(o, ones_bd):
    sq = o * o
    hi = sq.astype(BF16)
    lo = (sq - hi.astype(F32)).astype(BF16)
    tot = (jnp.dot(hi, ones_bd, preferred_element_type=F32)
           + jnp.dot(lo, ones_bd, preferred_element_type=F32))
    return tot * (1.0 / 64.0)


def _retention_chunk(q, k, v, s_bd, t):
    kexp = jnp.concatenate([k] * H_A, axis=0) * t["hmask"]
    sc = _dot_nt(q, kexp) * t["dcat"]
    vexp = jnp.concatenate([v] * H_A, axis=0) * t["vmask"]
    lhs = jnp.concatenate([q * t["qdec"], sc], axis=1)
    rhs = jnp.concatenate([s_bd, vexp], axis=0)
    o = _dot(lhs, rhs)
    s_new = t["cdec"] * s_bd + t["bd_a"] * _dot_tn(k * t["kdec"], v)
    return o, s_new


def _swa_block(q, k_new, v_new, k_prev, v_prev, sink_col, valid):
    bq = q.shape[0]
    bias = jnp.where(valid, 0.0, -jnp.inf).astype(F32)
    bias = jnp.concatenate([bias] * G_B, axis=0)
    outs = []
    for kv in range(KV_B):
        lo = kv * HD_B
        qs = jnp.concatenate(
            [q[:, (kv * G_B + g) * HD_B:(kv * G_B + g + 1) * HD_B] for g in range(G_B)], axis=0)
        qs = qs * (HD_B ** -0.5)
        keys = jnp.concatenate([k_prev[kv], k_new[:, lo:lo + HD_B]], axis=0)
        vals = jnp.concatenate([v_prev[kv], v_new[:, lo:lo + HD_B]], axis=0)
        s = _dot_nt(qs, keys) + bias
        sk = sink_col[kv]
        m = jnp.maximum(jnp.max(s, axis=1, keepdims=True), sk)
        e = jnp.exp(s - m)
        vext = jnp.concatenate([vals, jnp.ones_like(vals)], axis=1)
        oe = _dot(e, vext)
        den = oe[:, HD_B:] + jnp.exp(sk - m)
        o = oe[:, :HD_B] / den
        outs.append(jnp.concatenate([o[g * bq:(g + 1) * bq] for g in range(G_B)], axis=1))
    return jnp.concatenate(outs, axis=1)


def _swa_block_t(q, k_new, v_new, k_prev, v_prev, sink_col, valid_t):
    bq = q.shape[0]
    bias = jnp.where(valid_t, 0.0, -jnp.inf).astype(F32)
    bias = jnp.concatenate([bias] * G_B, axis=1)
    outs = []
    for kv in range(KV_B):
        lo = kv * HD_B
        qs = jnp.concatenate(
            [q[:, (kv * G_B + g) * HD_B:(kv * G_B + g + 1) * HD_B] for g in range(G_B)], axis=0)
        qs = qs * (HD_B ** -0.5)
        keys = jnp.concatenate([k_prev[kv], k_new[:, lo:lo + HD_B]], axis=0)
        vals = jnp.concatenate([v_prev[kv], v_new[:, lo:lo + HD_B]], axis=0)
        s = _dot_nt(keys, qs) + bias
        sk = sink_col[kv]
        m = jnp.maximum(jnp.max(s, axis=0, keepdims=True), sk)
        e = jnp.exp(s - m)
        vext = jnp.concatenate([vals, jnp.ones_like(vals)], axis=1)
        oe = _dot_tn(vext, e)
        o_t = oe[:HD_B] / (oe[HD_B:] + jnp.exp(sk - m))
        for pr in range(G_B // 2):
            pair = jnp.concatenate([o_t[:, (2 * pr) * bq:(2 * pr + 1) * bq],
                                    o_t[:, (2 * pr + 1) * bq:(2 * pr + 2) * bq]], axis=0)
            outs.append(pair.T)
    return jnp.concatenate(outs, axis=1)


def _swa_valid_t(bq, lim):
    key = lax.broadcasted_iota(jnp.int32, (WINDOW + bq, bq), 0)
    qry = lax.broadcasted_iota(jnp.int32, (WINDOW + bq, bq), 1)
    ok = jnp.where(key >= qry, jnp.where(key <= qry + WINDOW, 1, 0), 0)
    ok = jnp.where(key >= lim, ok, 0)
    return ok > 0


def _pool_tile(ext, pos0, wpool_bd, pool_scale):
    return _dot(_pool_sums(ext, pos0), wpool_bd) * pool_scale


def _pool_sums(ext, pos0):
    m = ext.shape[0] - 16
    s2 = ext[1:] + ext[:-1]
    s4 = s2[2:] + s2[:-2]
    s8 = s4[4:] + s4[:-4]
    s16 = s8[8:] + s8[:-8]
    u = ext[16:]
    lane = lax.broadcasted_iota(jnp.int32, (m, W_C), 1)
    grp = lane // GC
    win = jnp.where(grp == 0, s2[15:15 + m],
                    jnp.where(grp == 1, s4[13:13 + m],
                              jnp.where(grp == 2, s8[9:9 + m], s16[1:1 + m])))
    wlen = jnp.where(grp == 0, 2, jnp.where(grp == 1, 4, jnp.where(grp == 2, 8, 16)))
    pos = pos0 + lax.broadcasted_iota(jnp.int32, (m, W_C), 0)
    cnt = jnp.minimum(pos + 1, wlen).astype(F32)
    return win / cnt - u


HG_BLOCK = 128
HG_LEVELS = 7
HG_MXU_LEVELS = 3


def _hgrn_block_tree(q, k, v, lf, st_bd, t, ones_bd, bd_d):
    n = HG_BLOCK
    hi = lf.astype(BF16)
    r1 = lf - hi.astype(F32)
    mid = r1.astype(BF16)
    lo = (r1 - mid.astype(F32)).astype(BF16)
    amat = t["hg_a"]
    amat2 = jnp.concatenate([amat, amat], axis=1)
    ex = (jnp.dot(amat2, jnp.concatenate([hi, mid], axis=0), preferred_element_type=F32)
          + jnp.dot(amat, lo, preferred_element_type=F32))
    b = ex[0:n]
    blast = jnp.broadcast_to(b[n - 1:n], (n, W_D))
    qdec = q * jnp.exp(b)
    kdec = k * jnp.exp(blast - b)
    row = lax.broadcasted_iota(jnp.int32, (n, W_D), 0)
    lane = lax.broadcasted_iota(jnp.int32, (n, LANES), 1)
    lvl = t["hg_lvl"]
    xs = [jnp.zeros((n, n), F32) for _ in range(H_D)]
    for i in range(HG_LEVELS):
        s = 1 << i
        if i < HG_MXU_LEVELS:
            ex_s = ex[(1 + i) * n:(2 + i) * n]
        else:
            bmid = jnp.concatenate(
                [jnp.broadcast_to(b[j * 2 * s + s - 1:j * 2 * s + s], (2 * s, W_D)) for j in range(n // (2 * s))],
                axis=0)
            d = b - bmid
            ex_s = jnp.minimum(d, -d)
        second = (row // s) % 2 == 1
        p = jnp.where(second, q, k) * jnp.exp(ex_s)
        qt = jnp.where(second, p, 0.0).astype(BF16)
        kt = jnp.where(second, 0.0, p).astype(BF16)
        for pr in range(H_D // 2):
            c0 = pr * LANES
            ktp = kt[:, c0:c0 + LANES]
            zero = jnp.zeros((), BF16)
            kst = jnp.concatenate([jnp.where((lane // DK_D) == 0, ktp, zero),
                                   jnp.where((lane // DK_D) == 1, ktp, zero)], axis=0)
            x = lax.dot_general(qt[:, c0:c0 + LANES], kst, (((1,), (1,)), ((), ())),
                                preferred_element_type=F32)
            for hh in range(2):
                h = 2 * pr + hh
                xs[h] = jnp.where(lvl == i, x[:, hh * n:(hh + 1) * n], xs[h])
    vb = v.astype(BF16)
    vexp = jnp.concatenate([vb] * H_D, axis=0) * t["hg_vmask"]
    sc = jnp.concatenate(xs, axis=1).astype(BF16)
    o = jnp.dot(sc, vexp, preferred_element_type=F32)
    o = o + _dot(q * k, ones_bd) * v
    o = o + _dot_nt(qdec, st_bd)
    st_new = jnp.exp(b[n - 1:n]) * st_bd + bd_d * _dot_tn(v, kdec)
    return o, st_new


def _hgrn_tables():
    n = HG_BLOCK
    t = np.arange(n)[:, None]
    u = np.arange(n)[None, :]
    mats = [(u <= t)]
    for i in range(HG_MXU_LEVELS):
        s = 1 << i
        midp = (t // (2 * s)) * 2 * s + s - 1
        second = (t % (2 * s)) >= s
        mats.append(np.where(second, (u > midp) & (u <= t), (u > t) & (u <= midp)))
    amat = np.concatenate(mats, axis=0).astype(np.float32)
    x = np.bitwise_xor(t, u)
    lvl = np.where(u < t, np.floor(np.log2(np.maximum(x, 1))).astype(np.int32), -1).astype(np.int32)
    head_of_row = np.repeat(np.arange(H_D), n)
    vmask = (head_of_row[:, None] == (np.arange(W_D) // DV_D)[None, :]).astype(np.float32)
    return dict(hg_a=jnp.asarray(amat, BF16), hg_lvl=jnp.asarray(lvl), hg_vmask=jnp.asarray(vmask, BF16))


HG_TABLE_NAMES = ("hg_a", "hg_lvl", "hg_vmask")


def _lower_bound_row(lower_bounds, layer):
    x = lower_bounds
    mx = jnp.max(x, axis=0, keepdims=True)
    e = jnp.exp(x - mx)
    tot = jnp.sum(e, axis=0, keepdims=True)
    if layer == 0:
        return jnp.zeros_like(tot)
    acc = e[1:2]
    for i in range(2, layer + 1):
        acc = acc + e[i:i + 1]
    return acc / tot


def _swa_valid(bq, lim):
    row = lax.broadcasted_iota(jnp.int32, (bq, WINDOW + bq), 0)
    col = lax.broadcasted_iota(jnp.int32, (bq, WINDOW + bq), 1)
    ok = jnp.where(col >= row, jnp.where(col <= row + WINDOW, 1, 0), 0)
    ok = jnp.where(col >= lim, ok, 0)
    return ok > 0


def _rope_tables(pos):
    pos = pos.astype(F32)[:, None]
    half_a = DK_A // 2
    inv_a = jnp.power(RET_THETA, -jnp.arange(half_a, dtype=F32) / half_a)
    ang = pos * inv_a[None, :]
    ca, sa = lax.optimization_barrier((jnp.cos(ang), jnp.sin(ang)))
    cos_a = jnp.tile(jnp.concatenate([ca, ca], -1), (1, H_A))
    sin_a = jnp.tile(jnp.concatenate([-sa, sa], -1), (1, H_A))
    half_b = ROT_B // 2
    inv_b = jnp.power(ROPE_THETA, -jnp.arange(half_b, dtype=F32) / half_b)
    angb = pos * inv_b[None, :]
    t = pos.shape[0]
    one = jnp.ones((t, HD_B - ROT_B), F32)
    zero = jnp.zeros((t, HD_B - ROT_B), F32)
    cb, sb = lax.optimization_barrier((jnp.cos(angb), jnp.sin(angb)))
    cos_b = jnp.tile(jnp.concatenate([cb, cb, one], -1), (1, 2))
    sin_b = jnp.tile(jnp.concatenate([-sb, sb, zero], -1), (1, 2))
    return cos_a, sin_a, cos_b, sin_b


def _ret_tables(c):
    lg = jnp.log1p(-jnp.power(2.0, -5.0 - jnp.arange(H_A, dtype=F32)))
    idx = jnp.arange(c, dtype=F32)
    rel = idx[:, None] - idx[None, :]
    dmask = jnp.where(rel[None] >= 0, jnp.exp(jnp.maximum(rel, 0.0)[None] * lg[:, None, None]), 0.0)
    dcat = jnp.transpose(dmask, (1, 0, 2)).reshape(c, H_A * c)
    head_of_row = jnp.repeat(jnp.arange(H_A), c)
    hmask = (head_of_row[:, None] == (jnp.arange(128) // DK_A)[None, :]).astype(F32)
    vmask = (head_of_row[:, None] == (jnp.arange(W_A) // DV_A)[None, :]).astype(F32)
    lg_lane = jnp.repeat(lg, DK_A)
    qdec = jnp.exp((idx[:, None] + 1.0) * lg_lane[None, :])
    kdec = jnp.exp((c - 1.0 - idx)[:, None] * lg_lane[None, :])
    bd_a = ((jnp.arange(128) // DK_A)[:, None] == (jnp.arange(W_A) // DV_A)[None, :]).astype(F32)
    cdec = jnp.exp(c * lg_lane)[:, None] * bd_a
    return dict(hmask=hmask, dcat=dcat, vmask=vmask, qdec=qdec, kdec=kdec, cdec=cdec, bd_a=bd_a)


RET_TABLE_NAMES = ("hmask", "dcat", "vmask", "qdec", "kdec", "cdec", "bd_a")


def _block_ones_64():
    g = jnp.arange(256) // 64
    return (g[:, None] == g[None, :])


def _pool_weight_bd(w_pool_l):
    out = jnp.zeros((W_C, W_C), F32)
    for g in range(len(POOL_WINDOWS)):
        out = out.at[g * GC:(g + 1) * GC, g * GC:(g + 1) * GC].set(w_pool_l[g])
    return out.astype(BF16)


def _prompt_layer_body(layer, tt, *refs):
    (x_ref, cosa_ref, sina_ref, cosb_ref, sinb_ref,
     hmask_ref, dcat_ref, vmask_ref, qdec_ref, kdec_ref, cdec_ref, bda_ref,
     hga_ref, hglvl_ref, hgvm_ref,
     ones_ref, bdd_ref, win_ref, wbr_ref, wout_ref, gpre_ref, gpost_ref, sink_ref,
     wpool_ref, pscale_ref, ghgrn_ref, lbnd_ref,
     xo_ref, ret_ref, ko_ref, vo_ref, poolo_ref, hgo_ref,
     hb_ref, y_ref, sbd_ref, stbd_ref, kprev_ref, vprev_ref, ext_ref) = refs
    t_i = pl.program_id(1)
    n_t = pl.num_programs(1)

    @pl.when(t_i == 0)
    def _():
        sbd_ref[...] = jnp.zeros_like(sbd_ref)
        stbd_ref[...] = jnp.zeros_like(stbd_ref)
        kprev_ref[...] = jnp.zeros_like(kprev_ref)
        vprev_ref[...] = jnp.zeros_like(vprev_ref)
        ext_ref[0:16, :] = jnp.zeros((16, W_C), F32)

    x = x_ref[0]
    hb_ref[...] = (_rms(x) * gpre_ref[...]).astype(BF16)
    ones_bd = ones_ref[...]
    rt = dict(hmask=hmask_ref[...], dcat=dcat_ref[...], vmask=vmask_ref[...], qdec=qdec_ref[...],
              kdec=kdec_ref[...], cdec=cdec_ref[...], bd_a=bda_ref[...])
    nblk = tt // RET_CHUNK

    pa = jnp.dot(hb_ref[...], win_ref[:, QA:ZA + W_A], preferred_element_type=F32)
    qa = _rope(pa[:, QA:QA + 128], cosa_ref[...], sina_ref[...], DK_A // 2, DK_A)
    ka = _rope(pa[:, KA:KA + 128], cosa_ref[...], sina_ref[...], DK_A // 2, DK_A) * (DK_A ** -0.5)
    va = pa[:, VA:VA + W_A]
    s_bd = sbd_ref[...]
    oa = []
    for c in range(nblk):
        r = slice(c * RET_CHUNK, (c + 1) * RET_CHUNK)
        o_c, s_bd = _retention_chunk(qa[r], ka[r], va[r], s_bd, rt)
        oa.append(o_c)
    sbd_ref[...] = s_bd
    oa = jnp.concatenate(oa, axis=0)
    oa = oa * lax.rsqrt(_group_mean_sq(oa, ones_bd) + EPS)
    y_ref[:, Y_OFF[0]:Y_OFF[0] + W_A] = (oa * _silu(pa[:, ZA:ZA + W_A])).astype(BF16)

    pb = jnp.dot(hb_ref[...], win_ref[:, QB:ZB + W_B], preferred_element_type=F32)
    cosb, sinb = cosb_ref[...], sinb_ref[...]
    qb = _rope(pb[:, 0:W_B], jnp.concatenate([cosb] * 4, axis=1), jnp.concatenate([sinb] * 4, axis=1),
               ROT_B // 2, HD_B)
    kb = _rope(pb[:, KB - QB:KB - QB + 128], cosb, sinb, ROT_B // 2, HD_B)
    vb = pb[:, VB - QB:VB - QB + 128]
    sink_col = sink_ref[...]
    ob = []
    k_prev, v_prev = kprev_ref[...], vprev_ref[...]
    for c in range(nblk):
        r = slice(c * WINDOW, (c + 1) * WINDOW)
        lim = jnp.where(t_i == 0, WINDOW, 0) if c == 0 else 0
        valid = _swa_valid_t(WINDOW, lim)
        ob.append(_swa_block_t(qb[r], kb[r], vb[r], [k_prev[:, 0:HD_B], k_prev[:, HD_B:]],
                             [v_prev[:, 0:HD_B], v_prev[:, HD_B:]], sink_col, valid))
        k_prev, v_prev = kb[r], vb[r]
    kprev_ref[...] = k_prev
    vprev_ref[...] = v_prev
    ob = jnp.concatenate(ob, axis=0)
    y_ref[:, Y_OFF[1]:Y_OFF[1] + W_B] = (ob * _silu(pb[:, ZB - QB:ZB - QB + W_B])).astype(BF16)

    pc = jnp.dot(hb_ref[...], win_ref[:, UC:ZC + W_C], preferred_element_type=F32)
    ext_ref[16:16 + tt, :] = pc[:, 0:W_C]
    oc = _pool_tile(ext_ref[...], t_i * tt, wpool_ref[...], pscale_ref[...])
    y_ref[:, Y_OFF[2]:Y_OFF[2] + W_C] = (oc * _silu(pc[:, W_C:2 * W_C])).astype(BF16)

    ext_ref[0:16, :] = ext_ref[tt:tt + 16, :]

    pd = jnp.dot(hb_ref[...], win_ref[:, QD:ZD + W_D], preferred_element_type=F32)
    lb = _lower_bound_row(lbnd_ref[...], layer)
    ht = dict(hg_a=hga_ref[...], hg_lvl=hglvl_ref[...], hg_vmask=hgvm_ref[...])
    fg = lb + (1.0 - lb) * _sigmoid(pd[:, 256:512])
    qh = _silu(pd[:, 0:256])
    kh = 1.0 - fg
    lfh = jnp.log(fg)
    vh = pd[:, 512:768]
    st_bd = stbd_ref[...]
    od = []
    for c in range(tt // HG_BLOCK):
        r = slice(c * HG_BLOCK, (c + 1) * HG_BLOCK)
        o_c, st_bd = _hgrn_block_tree(qh[r], kh[r], vh[r], lfh[r], st_bd, ht, ones_bd, bdd_ref[...])
        od.append(o_c)
    od = jnp.concatenate(od, axis=0)
    stbd_ref[...] = st_bd
    od = od * lax.rsqrt(_group_mean_sq(od, ones_bd) + EPS) * ghgrn_ref[...]
    y_ref[:, Y_OFF[3]:Y_OFF[3] + W_D] = (od * _silu(pd[:, 768:1024])).astype(BF16)

    merged = None
    for i in range(4):
        g = _sigmoid(jnp.dot(hb_ref[...], win_ref[:, GL + i * D_MODEL:GL + (i + 1) * D_MODEL],
                             preferred_element_type=F32))
        term = g * jnp.dot(y_ref[:, Y_OFF[i]:Y_OFF[i] + BRANCH_WIDTHS[i]],
                           wbr_ref[Y_OFF[i]:Y_OFF[i] + BRANCH_WIDTHS[i], :], preferred_element_type=F32)
        merged = term if merged is None else merged + term
    out = jnp.dot(merged.astype(BF16), wout_ref[...], preferred_element_type=F32)
    xo_ref[0] = x + _rms(out) * gpost_ref[...]

    @pl.when(t_i == n_t - 1)
    def _():
        s = sbd_ref[...]
        ret_ref[0] = s[:, 0:64] + s[:, 64:128] + s[:, 128:192] + s[:, 192:256]
        st = stbd_ref[...]
        hgo_ref[0] = (st[0:64] + st[64:128] + st[128:192] + st[192:256]).T
        ko_ref[0] = kprev_ref[...]
        vo_ref[0] = vprev_ref[...]
        poolo_ref[0] = ext_ref[pl.ds(1, POOL_HIST), :]


def _const_spec(shape):
    nd = len(shape)
    return pl.BlockSpec(shape, lambda b, t: (0,) * nd, pipeline_mode=pl.Buffered(1))


def _prompt_layer(layer, x, tabs, consts, w_in, w_br, w_out, g_pre, g_post, sink_col, wpool_bd, pscale,
                  g_hgrn, lower_bounds):
    bsz, seq, _ = x.shape
    tt = PROMPT_TILE
    nt = seq // tt
    cos_a, sin_a, cos_b, sin_b = tabs
    rt, ones_bd, bd_d, hg = consts
    tab_spec = pl.BlockSpec((tt, LANES), lambda b, t: (t, 0))
    in_arrays = [x, cos_a, sin_a, cos_b, sin_b] + [rt[n] for n in RET_TABLE_NAMES] + [
        hg[n] for n in HG_TABLE_NAMES] + [
        ones_bd, bd_d, w_in, w_br, w_out, g_pre, g_post, sink_col, wpool_bd, pscale, g_hgrn, lower_bounds]
    in_specs = [pl.BlockSpec((1, tt, D_MODEL), lambda b, t: (b, t, 0)), tab_spec, tab_spec, tab_spec, tab_spec]
    in_specs += [_const_spec(a.shape) for a in in_arrays[5:]]
    out_shape = (
        jax.ShapeDtypeStruct((bsz, seq, D_MODEL), F32),
        jax.ShapeDtypeStruct((bsz, H_A * DK_A, DV_A), F32),
        jax.ShapeDtypeStruct((bsz, WINDOW, KV_B * HD_B), F32),
        jax.ShapeDtypeStruct((bsz, WINDOW, KV_B * HD_B), F32),
        jax.ShapeDtypeStruct((bsz, POOL_HIST, W_C), F32),
        jax.ShapeDtypeStruct((bsz, H_D * DK_D, DV_D), F32),
    )
    out_specs = (
        pl.BlockSpec((1, tt, D_MODEL), lambda b, t: (b, t, 0)),
        pl.BlockSpec((1, H_A * DK_A, DV_A), lambda b, t: (b, 0, 0)),
        pl.BlockSpec((1, WINDOW, KV_B * HD_B), lambda b, t: (b, 0, 0)),
        pl.BlockSpec((1, WINDOW, KV_B * HD_B), lambda b, t: (b, 0, 0)),
        pl.BlockSpec((1, POOL_HIST, W_C), lambda b, t: (b, 0, 0)),
        pl.BlockSpec((1, H_D * DK_D, DV_D), lambda b, t: (b, 0, 0)),
    )
    scratch = [
        pltpu.VMEM((tt, D_MODEL), BF16),
        pltpu.VMEM((tt, W_MIX), BF16),
        pltpu.VMEM((H_A * DK_A, W_A), F32),
        pltpu.VMEM((W_D, H_D * DK_D), F32),
        pltpu.VMEM((WINDOW, KV_B * HD_B), F32),
        pltpu.VMEM((WINDOW, KV_B * HD_B), F32),
        pltpu.VMEM((16 + tt, W_C), F32),
    ]
    return pl.pallas_call(
        functools.partial(_prompt_layer_body, layer, tt),
        grid=(bsz, nt),
        in_specs=in_specs,
        out_specs=out_specs,
        out_shape=out_shape,
        scratch_shapes=scratch,
        compiler_params=pltpu.CompilerParams(
            dimension_semantics=("arbitrary", "arbitrary"), vmem_limit_bytes=VMEM_LIMIT),
        name=f"prompt_layer{layer}",
    )(*in_arrays)


def _sample_inproj_body(x_ref, gpre_ref, w_ref, o_ref):
    hb = (_rms(x_ref[...]) * gpre_ref[...]).astype(BF16)
    o_ref[...] = jnp.dot(hb, w_ref[...], preferred_element_type=F32)


def _sample_inproj(layer, x2, g_pre, w_in):
    n = x2.shape[0]
    cb = 512
    return pl.pallas_call(
        _sample_inproj_body,
        grid=(D_IN // cb,),
        in_specs=[pl.BlockSpec((n, D_MODEL), lambda j: (0, 0)),
                  pl.BlockSpec((1, D_MODEL), lambda j: (0, 0)),
                  pl.BlockSpec((D_MODEL, cb), lambda j: (0, j))],
        out_specs=pl.BlockSpec((n, cb), lambda j: (0, j)),
        out_shape=jax.ShapeDtypeStruct((n, D_IN), F32),
        compiler_params=pltpu.CompilerParams(dimension_semantics=("arbitrary",), vmem_limit_bytes=VMEM_LIMIT),
        name=f"sample_inproj{layer}",
    )(x2, g_pre, w_in)


SAMPLE_SEQS_PER_STEP = 8
SAMPLE_TABLE_NAMES = ("hmask", "dcat", "vmask", "qdec", "kdec", "cdec", "hm_a", "hm_d")


def _sample_tables(tq, bb):
    m = tq * bb
    lg = jnp.log1p(-jnp.power(2.0, -5.0 - jnp.arange(H_A, dtype=F32)))
    t = (jnp.arange(m) % tq).astype(F32)
    seq = jnp.arange(m) // tq
    rel = t[:, None] - t[None, :]
    same = (seq[:, None] == seq[None, :]) & (rel >= 0)
    dmask = jnp.where(same[None], jnp.exp(jnp.maximum(rel, 0.0)[None] * lg[:, None, None]), 0.0)
    dcat = jnp.transpose(dmask, (1, 0, 2)).reshape(m, H_A * m)
    head_of_row = jnp.repeat(jnp.arange(H_A), m)
    hmask = (head_of_row[:, None] == (jnp.arange(128) // DK_A)[None, :]).astype(F32)
    vmask = (head_of_row[:, None] == (jnp.arange(W_A) // DV_A)[None, :]).astype(F32)
    lg_lane = jnp.repeat(lg, DK_A)
    qdec = jnp.exp((t[:, None] + 1.0) * lg_lane[None, :])
    kdec = jnp.exp((tq - 1.0 - t)[:, None] * lg_lane[None, :])
    cdec = jnp.broadcast_to(jnp.exp(tq * lg_lane)[:, None], (H_A * DK_A, DV_A))
    head_of_small = jnp.repeat(jnp.arange(4), tq)
    hm_a = (head_of_small[:, None] == (jnp.arange(H_A * DK_A) // DK_A)[None, :]).astype(F32)
    hm_d = (head_of_small[:, None] == (jnp.arange(H_D * DK_D) // DK_D)[None, :]).astype(F32)
    return dict(hmask=hmask, dcat=dcat, vmask=vmask, qdec=qdec, kdec=kdec, cdec=cdec, hm_a=hm_a, hm_d=hm_d)


def _sample_mixer_body(layer, tq, bb, *refs):
    (p_ref, cosa_ref, sina_ref, cosb_ref, sinb_ref,
     hmask_ref, dcat_ref, vmask_ref, qdec_ref, kdec_ref, cdec_ref, hma_ref, hmd_ref,
     ones_ref, sink_ref, wpool_ref, pscale_ref, ghgrn_ref, lbnd_ref,
     sret_ref, ck_ref, cv_ref, hist_ref, shg_ref,
     y_ref, reto_ref, ko_ref, vo_ref, poolo_ref, hgo_ref) = refs
    ones_bd = ones_ref[...]
    cosa, sina, cosb, sinb = cosa_ref[...], sina_ref[...], cosb_ref[...], sinb_ref[...]
    lb = _lower_bound_row(lbnd_ref[...], layer)
    valid = _swa_valid(tq, 0)
    m = bb * tq
    p = p_ref[...]

    qa = _rope(p[:, QA:QA + 128], cosa, sina, DK_A // 2, DK_A)
    ka = _rope(p[:, KA:KA + 128], cosa, sina, DK_A // 2, DK_A) * (DK_A ** -0.5)
    va = p[:, VA:VA + W_A]
    kexp = jnp.concatenate([ka] * H_A, axis=0) * hmask_ref[...]
    sc = _dot_nt(qa, kexp) * dcat_ref[...]
    vexp = jnp.concatenate([va] * H_A, axis=0) * vmask_ref[...]
    oa_intra = _dot(sc, vexp)
    qa_dec = qa * qdec_ref[...]
    ka_dec = ka * kdec_ref[...]

    qb = _rope(p[:, QB:QB + W_B], jnp.concatenate([cosb] * 4, axis=1), jnp.concatenate([sinb] * 4, axis=1),
               ROT_B // 2, HD_B)
    kb = _rope(p[:, KB:KB + 128], cosb, sinb, ROT_B // 2, HD_B)
    vb = p[:, VB:VB + 128]

    chunk = math.gcd(tq, HGRN_CHUNK)
    fg = lb + (1.0 - lb) * _sigmoid(p[:, FD:FD + 256])
    qh = _silu(p[:, QD:QD + 256])
    kh = 1.0 - fg
    vh = p[:, ID:ID + 256]
    rc = lax.broadcasted_iota(jnp.int32, (m, W_D), 0) % chunk
    bcum = jnp.log(fg)
    sh = 1
    while sh < chunk:
        bcum = bcum + jnp.where(rc >= sh, _shift_rows(bcum, sh), 0.0)
        sh *= 2
    od_intra = _dot(qh * kh, ones_bd) * vh
    for j in range(1, chunk):
        a = jnp.where(rc >= j, qh * _shift_rows(kh, j) * jnp.exp(bcum - _shift_rows(bcum, j)), 0.0)
        od_intra = od_intra + _dot(a, ones_bd) * _shift_rows(vh, j)
    blast = bcum
    for j in range(1, chunk):
        blast = jnp.where(rc == chunk - 1 - j, pltpu.roll(bcum, m - j, 0), blast)
    qh_dec = qh * jnp.exp(bcum)
    kh_dec = kh * jnp.exp(blast - bcum)
    dech = jnp.exp(blast)
    d_hi = dech.astype(BF16)
    d_r1 = dech - d_hi.astype(F32)
    d_mid = d_r1.astype(BF16)
    d_lo = (d_r1 - d_mid.astype(F32)).astype(BF16)
    ones3 = jnp.ones((3, DV_D), BF16)

    hm_a = hma_ref[...]
    hm_d = hmd_ref[...]
    cdec = cdec_ref[...]

    def state_step(qd_b, kd_b, v_b, s, decmat, hm):
        r = _dot(jnp.concatenate([qd_b] * 4, axis=0) * hm, s)
        o = jnp.concatenate([r[h * tq:(h + 1) * tq] for h in range(4)], axis=1)
        vr = jnp.concatenate([v_b[:, h * 64:(h + 1) * 64] for h in range(4)], axis=0)
        s_new = decmat * s + _dot_tn(jnp.concatenate([kd_b] * 4, axis=0) * hm, vr)
        return o, s_new

    oa_inter, od_inter, ob, pooled = [], [], [], []
    for b in range(bb):
        r = slice(b * tq, (b + 1) * tq)

        o_b, s_new = state_step(qa_dec[r], ka_dec[r], va[r], sret_ref[0, b].reshape(H_A * DK_A, DV_A), cdec, hm_a)
        oa_inter.append(o_b)
        reto_ref[b] = s_new.reshape(H_A, DK_A, DV_A)

        k_prev, v_prev = ck_ref[0, b], cv_ref[0, b]
        ob.append(_swa_block(qb[r], kb[r], vb[r], [k_prev[:, 0:HD_B], k_prev[:, HD_B:]],
                             [v_prev[:, 0:HD_B], v_prev[:, HD_B:]], sink_ref[...], valid))
        ko_ref[b] = jnp.concatenate([k_prev[tq:], kb[r]], axis=0)
        vo_ref[b] = jnp.concatenate([v_prev[tq:], vb[r]], axis=0)

        ext = jnp.concatenate([jnp.zeros((1, W_C), F32), hist_ref[0, b], p[r, UC:UC + W_C]], axis=0)
        pooled.append(_pool_sums(ext, PAST_LEN))
        poolo_ref[b] = ext[tq + 1:tq + 16]

        r0 = b * tq
        dec3 = jnp.concatenate([d_hi[r0:r0 + 1], d_mid[r0:r0 + 1], d_lo[r0:r0 + 1]], axis=0)
        decmat = lax.dot_general(dec3, ones3, (((0,), (0,)), ((), ())), preferred_element_type=F32)
        o_b, st_new = state_step(qh_dec[r], kh_dec[r], vh[r], shg_ref[0, b].reshape(H_D * DK_D, DV_D),
                                 decmat, hm_d)
        od_inter.append(o_b)
        hgo_ref[b] = st_new.reshape(H_D, DK_D, DV_D)

    oa = oa_intra + jnp.concatenate(oa_inter, axis=0)
    oa = oa * lax.rsqrt(_group_mean_sq(oa, ones_bd) + EPS)
    ya = oa * _silu(p[:, ZA:ZA + W_A])
    yb = jnp.concatenate(ob, axis=0) * _silu(p[:, ZB:ZB + W_B])
    oc = _dot(jnp.concatenate(pooled, axis=0), wpool_ref[...]) * pscale_ref[...]
    yc = oc * _silu(p[:, ZC:ZC + W_C])
    od = od_intra + jnp.concatenate(od_inter, axis=0)
    od = od * lax.rsqrt(_group_mean_sq(od, ones_bd) + EPS) * ghgrn_ref[...]
    yd = od * _silu(p[:, ZD:ZD + W_D])
    y_ref[...] = jnp.concatenate([ya, yb, yc, yd], axis=1).astype(BF16)


def _sample_mixer(layer, proj, tq, tabs, consts, sink_col, wpool_bd, pscale, g_hgrn, lower_bounds,
                  s_ret, c_k, c_v, hist, s_hg):
    nb = proj.shape[0] // tq
    bb = SAMPLE_SEQS_PER_STEP
    cos_a, sin_a, cos_b, sin_b = tabs
    st, ones_bd = consts
    const_arrays = [cos_a, sin_a, cos_b, sin_b] + [st[n] for n in SAMPLE_TABLE_NAMES] + [
        ones_bd, sink_col, wpool_bd, pscale, g_hgrn, lower_bounds]

    def cspec(a):
        nd = a.ndim
        return pl.BlockSpec(a.shape, lambda i: (0,) * nd)

    def in_state_spec(a):
        nd = a.ndim
        return pl.BlockSpec((1, bb) + a.shape[2:], lambda i: (layer, i) + (0,) * (nd - 2))

    def out_state_spec(shape):
        nd = len(shape)
        return pl.BlockSpec((bb,) + shape[1:], lambda i: (i,) + (0,) * (nd - 1))

    state_arrays = [s_ret, c_k, c_v, hist, s_hg]
    state_out_shapes = [a.shape[1:] for a in state_arrays]
    out_shape = (jax.ShapeDtypeStruct((nb * tq, W_MIX), BF16),) + tuple(
        jax.ShapeDtypeStruct(s, F32) for s in state_out_shapes)
    return pl.pallas_call(
        functools.partial(_sample_mixer_body, layer, tq, bb),
        grid=(nb // bb,),
        in_specs=[pl.BlockSpec((bb * tq, GL), lambda i: (i, 0))] + [cspec(a) for a in const_arrays]
        + [in_state_spec(a) for a in state_arrays],
        out_specs=(pl.BlockSpec((bb * tq, W_MIX), lambda i: (i, 0)),) + tuple(
            out_state_spec(s) for s in state_out_shapes),
        out_shape=out_shape,
        compiler_params=pltpu.CompilerParams(dimension_semantics=("arbitrary",), vmem_limit_bytes=VMEM_LIMIT),
        name=f"sample_mixer{layer}",
    )(proj, *const_arrays, *state_arrays)


MERGE_COLS = 512


def _sample_merge_body(x_ref, y_ref, g0_ref, g1_ref, g2_ref, g3_ref, wbr_ref, wout_ref, gpost_ref, o_ref,
                       merged_ref):
    c = pl.program_id(0)
    merged = None
    for i, g_ref in enumerate((g0_ref, g1_ref, g2_ref, g3_ref)):
        term = _sigmoid(g_ref[...]) * jnp.dot(y_ref[:, Y_OFF[i]:Y_OFF[i] + BRANCH_WIDTHS[i]],
                                              wbr_ref[Y_OFF[i]:Y_OFF[i] + BRANCH_WIDTHS[i], :],
                                              preferred_element_type=F32)
        merged = term if merged is None else merged + term
    merged_ref[c] = merged.astype(BF16)

    @pl.when(c == pl.num_programs(0) - 1)
    def _():
        nchunk = D_MODEL // MERGE_COLS
        out = None
        for j in range(nchunk):
            part = jnp.dot(merged_ref[j], wout_ref[j * MERGE_COLS:(j + 1) * MERGE_COLS, :],
                           preferred_element_type=F32)
            out = part if out is None else out + part
        o_ref[...] = x_ref[...] + _rms(out) * gpost_ref[...]


def _sample_merge(layer, x2, y2, proj, w_br, w_out, g_post):
    n = x2.shape[0]
    nchunk = D_MODEL // MERGE_COLS
    gate_blk0 = GL // MERGE_COLS

    def gate_spec(i):
        return pl.BlockSpec((n, MERGE_COLS), lambda c: (0, gate_blk0 + i * nchunk + c))

    return pl.pallas_call(
        _sample_merge_body,
        grid=(nchunk,),
        in_specs=[pl.BlockSpec((n, D_MODEL), lambda c: (0, 0)),
                  pl.BlockSpec((n, W_MIX), lambda c: (0, 0)),
                  gate_spec(0), gate_spec(1), gate_spec(2), gate_spec(3),
                  pl.BlockSpec((W_MIX, MERGE_COLS), lambda c: (0, c)),
                  pl.BlockSpec((D_MODEL, D_MODEL), lambda c: (0, 0)),
                  pl.BlockSpec((1, D_MODEL), lambda c: (0, 0))],
        out_specs=pl.BlockSpec((n, D_MODEL), lambda c: (0, 0)),
        out_shape=jax.ShapeDtypeStruct(x2.shape, F32),
        scratch_shapes=[pltpu.VMEM((nchunk, n, MERGE_COLS), BF16)],
        compiler_params=pltpu.CompilerParams(dimension_semantics=("arbitrary",), vmem_limit_bytes=VMEM_LIMIT),
        name=f"sample_merge{layer}",
    )(x2, y2, proj, proj, proj, proj, w_br, w_out, g_post)


def kernel(x_prompt, x_sample, state_ret, cache_swa_k, cache_swa_v, state_pool, state_hgrn, w_in, w_branch,
           w_out, g_pre, g_post, attn_sink, w_pool, pool_scale, g_hgrn, lower_bounds):
    bsz, seq, _ = x_prompt.shape
    nb, tq, _ = x_sample.shape
    w_in_b = w_in.astype(BF16)
    w_br_b = w_branch.astype(BF16)
    w_out_b = w_out.astype(BF16)
    ones_bd = _block_ones_64().astype(BF16)
    bd_d = _block_ones_64().astype(F32)
    consts_p = (_ret_tables(RET_CHUNK), ones_bd, bd_d, _hgrn_tables())
    consts_s = (_sample_tables(tq, SAMPLE_SEQS_PER_STEP), ones_bd)
    tabs_p = _rope_tables(jnp.arange(seq))
    tabs_s = tuple(jnp.tile(a, (SAMPLE_SEQS_PER_STEP, 1)) for a in _rope_tables(PAST_LEN + jnp.arange(tq)))

    def sink_cols(sink_l, rows):
        return jnp.repeat(sink_l.reshape(KV_B, G_B), rows, axis=1)[..., None].astype(F32)

    cache_k4 = cache_swa_k.reshape(DEPTH, nb, WINDOW, KV_B * HD_B)
    cache_v4 = cache_swa_v.reshape(DEPTH, nb, WINDOW, KV_B * HD_B)
    xp = x_prompt
    xs = x_sample.reshape(nb * tq, D_MODEL)
    acc_p = [[] for _ in range(5)]
    acc_s = [[] for _ in range(5)]
    for l in range(DEPTH):
        wpool_bd = _pool_weight_bd(w_pool[l])
        pscale = pool_scale[l][None, :]
        ghg = g_hgrn[l][None, :]
        gpre = g_pre[l][None, :]
        gpost = g_post[l][None, :]

        xp, r_p, k_p, v_p, pool_p, h_p = _prompt_layer(
            l, xp, tabs_p, consts_p, w_in_b[l], w_br_b[l], w_out_b[l], gpre, gpost,
            jnp.swapaxes(sink_cols(attn_sink[l], WINDOW), 1, 2), wpool_bd, pscale, ghg, lower_bounds)
        for a, s in zip(acc_p, (r_p.reshape(bsz, H_A, DK_A, DV_A), k_p.reshape(bsz, WINDOW, KV_B, HD_B),
                                v_p.reshape(bsz, WINDOW, KV_B, HD_B), pool_p,
                                h_p.reshape(bsz, H_D, DK_D, DV_D))):
            a.append(s)

        proj = _sample_inproj(l, xs, gpre, w_in_b[l])
        y2, r_s, k_s, v_s, pool_s, h_s = _sample_mixer(
            l, proj, tq, tabs_s, consts_s, sink_cols(attn_sink[l], tq), wpool_bd, pscale, ghg, lower_bounds,
            state_ret, cache_k4, cache_v4, state_pool, state_hgrn)
        xs = _sample_merge(l, xs, y2, proj, w_br_b[l], w_out_b[l], gpost)
        for a, s in zip(acc_s, (r_s, k_s, v_s, pool_s, h_s)):
            a.append(s)

    outs_p = [jnp.stack(a) for a in acc_p]
    outs_s = [jnp.stack(a) for a in acc_s]
    outs_s[1] = outs_s[1].reshape(DEPTH, nb, WINDOW, KV_B, HD_B)
    outs_s[2] = outs_s[2].reshape(DEPTH, nb, WINDOW, KV_B, HD_B)
    return (xp, xs.reshape(nb, tq, D_MODEL), *outs_p, *outs_s)
```
